```python
import math
import jax, jax.numpy as jnp
from jax import lax
import numpy as np

D_MODEL = 1024
BATCH = 32
SEQ = 2048
DEPTH = 2

N_MIXERS = 2
NORM_EPS = 1e-6

CONV_WIDTH = 3

MLA_HEADS = 16
MLA_Q_LORA = 384
MLA_KV_LORA = 256
MLA_NOPE = 64
MLA_ROPE = 32
MLA_V = 64
ROPE_THETA = 10000.0
Q_BLOCK = 128

PEER_HEADS = 8
PEER_N_KEYS = 128
PEER_N_EXPERTS = PEER_N_KEYS * PEER_N_KEYS
PEER_TOPK = 16
PEER_D_QUERY = 256
PEER_D_HALF = PEER_D_QUERY // 2
PEER_TOKEN_BLOCK = 128

kernel_name = 'hybrid_conv_mla_peer_encoder'


def rmsnorm(x, g):
    x32 = x.astype(jnp.float32)
    y = x32 * lax.rsqrt(jnp.mean(x32 * x32, axis=-1, keepdims=True) + NORM_EPS)
    return (y * g.astype(jnp.float32)).astype(x.dtype)


def short_conv_mixer(xn, w_in, conv_w, w_out):
    proj = xn @ w_in
    b_gate, c_gate, h = jnp.split(proj, 3, axis=-1)
    u = c_gate * h
    pad = CONV_WIDTH // 2
    y = lax.conv_general_dilated(
        u, conv_w[:, None, :].astype(u.dtype), window_strides=(1,),
        padding=[(pad, pad)], dimension_numbers=('NWC', 'WIO', 'NWC'),
        feature_group_count=D_MODEL)
    return (b_gate * y) @ w_out


def apply_rope(x, cos, sin):
    x1, x2 = jnp.split(x, 2, axis=-1)
    return jnp.concatenate([x1 * cos - x2 * sin, x2 * cos + x1 * sin], axis=-1)


def mla_mixer(xn, w_in, q_norm, kv_norm, w_uq, w_ukv, w_o):
    bsz, seq, _ = xn.shape
    lat = xn @ w_in
    c_q, c_kv, k_rope = jnp.split(lat, [MLA_Q_LORA, MLA_Q_LORA + MLA_KV_LORA], axis=-1)
    c_q = rmsnorm(c_q, q_norm)
    c_kv = rmsnorm(c_kv, kv_norm)
    q = (c_q @ w_uq).reshape(bsz, seq, MLA_HEADS, MLA_NOPE + MLA_ROPE)
    q_nope, q_rope = jnp.split(q, [MLA_NOPE], axis=-1)
    kv = (c_kv @ w_ukv).reshape(bsz, seq, MLA_HEADS, MLA_NOPE + MLA_V)
    k_nope, v = jnp.split(kv, [MLA_NOPE], axis=-1)

    pos = jnp.arange(seq, dtype=jnp.float32)
    inv_freq = ROPE_THETA ** (-jnp.arange(0, MLA_ROPE, 2, dtype=jnp.float32) / MLA_ROPE)
    ang = pos[:, None] * inv_freq[None, :]
    cos = jnp.cos(ang).astype(xn.dtype)
    sin = jnp.sin(ang).astype(xn.dtype)
    q_rope = apply_rope(q_rope, cos[None, :, None, :], sin[None, :, None, :])
    k_rope = apply_rope(k_rope, cos[None], sin[None])

    scale = (MLA_NOPE + MLA_ROPE) ** -0.5
    n_blocks = seq // Q_BLOCK
    qn_b = q_nope.reshape(bsz, n_blocks, Q_BLOCK, MLA_HEADS, MLA_NOPE).transpose(1, 0, 2, 3, 4)
    qr_b = q_rope.reshape(bsz, n_blocks, Q_BLOCK, MLA_HEADS, MLA_ROPE).transpose(1, 0, 2, 3, 4)

    def attend(args):
        qn, qr = args
        s = (jnp.einsum('bqhd,bkhd->bhqk', qn, k_nope)
             + jnp.einsum('bqhr,bkr->bhqk', qr, k_rope))
        p = jax.nn.softmax(s.astype(jnp.float32) * scale, axis=-1).astype(v.dtype)
        return jnp.einsum('bhqk,bkhd->bqhd', p, v)

    o = lax.map(attend, (qn_b, qr_b))
    o = o.transpose(1, 0, 2, 3, 4).reshape(bsz, seq, MLA_HEADS * MLA_V)
    return o @ w_o


def peer_ffn(xn, w_q, k1, k2, u_tab, v_tab):
    bsz, seq, d = xn.shape
    t = bsz * seq
    xt = xn.reshape(t, d)
    q = (xt @ w_q).reshape(t, PEER_HEADS, 2, PEER_D_HALF)
    s1 = jnp.einsum('thd,hnd->thn', q[:, :, 0], k1).astype(jnp.float32)
    s2 = jnp.einsum('thd,hnd->thn', q[:, :, 1], k2).astype(jnp.float32)
    v1, i1 = lax.top_k(s1, PEER_TOPK)
    v2, i2 = lax.top_k(s2, PEER_TOPK)
    kk = PEER_TOPK * PEER_TOPK
    cand = (v1[..., :, None] + v2[..., None, :]).reshape(t, PEER_HEADS, kk)
    cidx = (i1[..., :, None] * PEER_N_KEYS + i2[..., None, :]).reshape(t, PEER_HEADS, kk)
    top_s, top_pos = lax.top_k(cand, PEER_TOPK)
    eidx = jnp.take_along_axis(cidx, top_pos, axis=-1)
    g = jax.nn.softmax(top_s, axis=-1).astype(xn.dtype)

    nb = t // PEER_TOKEN_BLOCK

    def expert_block(args):
        xb, ib, gb = args
        u = jnp.take(u_tab, ib, axis=0)
        a = jnp.einsum('thkd,td->thk', u, xb)
        h = jax.nn.gelu(a, approximate=False) * gb
        vv = jnp.take(v_tab, ib, axis=0)
        return jnp.einsum('thk,thkd->td', h, vv)

    out = lax.map(expert_block, (xt.reshape(nb, PEER_TOKEN_BLOCK, d),
                                 eidx.reshape(nb, PEER_TOKEN_BLOCK, PEER_HEADS, PEER_TOPK),
                                 g.reshape(nb, PEER_TOKEN_BLOCK, PEER_HEADS, PEER_TOPK)))
    return out.reshape(bsz, seq, d)


def setup_inputs(seed: int = 0) -> dict:
    key = jax.random.key(seed)
    ks = jax.random.split(key, 24)
    f32 = jnp.float32
    n_conv = (DEPTH + N_MIXERS - 1) // N_MIXERS
    n_mla = DEPTH // N_MIXERS

    def nrm(k, shape, scale):
        return jax.random.normal(k, shape, f32) * scale

    def gain(k, shape):
        return 1.0 + 0.02 * jax.random.normal(k, shape, f32)

    d = D_MODEL
    return {
        'x': jax.random.normal(ks[0], (BATCH, SEQ, d), f32),
        'norm_mix': gain(ks[1], (DEPTH, d)),
        'norm_ffn': gain(ks[2], (DEPTH, d)),
        'conv_w_in': nrm(ks[3], (n_conv, d, 3 * d), d ** -0.5),
        'conv_w': nrm(ks[4], (n_conv, CONV_WIDTH, d), CONV_WIDTH ** -0.5),
        'conv_w_out': nrm(ks[5], (n_conv, d, d), d ** -0.5),
        'mla_w_in': nrm(ks[6], (n_mla, d, MLA_Q_LORA + MLA_KV_LORA + MLA_ROPE), d ** -0.5),
        'mla_q_norm': gain(ks[7], (n_mla, MLA_Q_LORA)),
        'mla_kv_norm': gain(ks[8], (n_mla, MLA_KV_LORA)),
        'mla_w_uq': nrm(ks[9], (n_mla, MLA_Q_LORA, MLA_HEADS * (MLA_NOPE + MLA_ROPE)), MLA_Q_LORA ** -0.5),
        'mla_w_ukv': nrm(ks[10], (n_mla, MLA_KV_LORA, MLA_HEADS * (MLA_NOPE + MLA_V)), MLA_KV_LORA ** -0.5),
        'mla_w_o': nrm(ks[11], (n_mla, MLA_HEADS * MLA_V, d), (MLA_HEADS * MLA_V) ** -0.5),
        'peer_w_q': nrm(ks[12], (DEPTH, d, PEER_HEADS * PEER_D_QUERY), d ** -0.5),
        'peer_k1': nrm(ks[13], (DEPTH, PEER_HEADS, PEER_N_KEYS, PEER_D_HALF), PEER_D_HALF ** -0.5),
        'peer_k2': nrm(ks[14], (DEPTH, PEER_HEADS, PEER_N_KEYS, PEER_D_HALF), PEER_D_HALF ** -0.5),
        'peer_u': nrm(ks[15], (DEPTH, PEER_N_EXPERTS, d), d ** -0.5),
        'peer_v': nrm(ks[16], (DEPTH, PEER_N_EXPERTS, d), PEER_HEADS ** -0.5),
        'final_norm': gain(ks[17], (d,)),
    }


def reference(x, norm_mix, norm_ffn, conv_w_in, conv_w, conv_w_out,
              mla_w_in, mla_q_norm, mla_kv_norm, mla_w_uq, mla_w_ukv, mla_w_o,
              peer_w_q, peer_k1, peer_k2, peer_u, peer_v, final_norm):
    for i in range(DEPTH):
        j = i // N_MIXERS
        xn = rmsnorm(x, norm_mix[i])
        if i % N_MIXERS == 0:
            mix = short_conv_mixer(xn, conv_w_in[j], conv_w[j], conv_w_out[j])
        else:
            mix = mla_mixer(xn, mla_w_in[j], mla_q_norm[j], mla_kv_norm[j],
                            mla_w_uq[j], mla_w_ukv[j], mla_w_o[j])
        x = x + mix
        x = x + peer_ffn(rmsnorm(x, norm_ffn[i]), peer_w_q[i], peer_k1[i], peer_k2[i],
                         peer_u[i], peer_v[i])
    return rmsnorm(x, final_norm)
```

```python
import functools

import jax
import jax.numpy as jnp
from jax import lax
from jax.experimental import pallas as pl
from jax.experimental.pallas import tpu as pltpu

F32 = jnp.float32
BF16 = jnp.bfloat16

NORM_EPS = 1e-6
CONV_WIDTH = 3
MLA_HEADS = 16
MLA_Q_LORA = 384
MLA_KV_LORA = 256
MLA_NOPE = 64
MLA_ROPE = 32
MLA_V = 64
ROPE_THETA = 10000.0
PEER_HEADS = 8
PEER_N_KEYS = 128
PEER_TOPK = 16
PEER_D_HALF = 128

LANES = 128
SUBLANES = 8
HEAD_SLAB = 128
VMEM_LIMIT = 56 * 1024 * 1024

INV_SQRT2 = 0.7071067811865476
_NT = (((1,), (1,)), ((), ()))


def _tile(n, pref):
    t = min(n, pref)
    while n % t:
        t -= 1
    return t


def _params(*sem):
    return pltpu.CompilerParams(dimension_semantics=sem, vmem_limit_bytes=VMEM_LIMIT)


def _rms(x, g):
    return x * lax.rsqrt(jnp.mean(x * x, axis=-1, keepdims=True) + NORM_EPS) * g


def _conv_in_kernel(x_ref, g_ref, w_ref, u_ref, b_ref):
    d = x_ref.shape[1]
    xn = _rms(x_ref[...], g_ref[...]).astype(BF16)
    b_ref[...] = jnp.dot(xn, w_ref[:, :d], preferred_element_type=F32)
    c = jnp.dot(xn, w_ref[:, d:2 * d], preferred_element_type=F32)
    h = jnp.dot(xn, w_ref[:, 2 * d:], preferred_element_type=F32)
    u_ref[...] = c * h


def _conv_out_kernel(seq_tiles, u_ref, up_ref, un_ref, b_ref, x_ref, cw_ref, w_ref, o_ref):
    i = pl.program_id(0)
    tm = u_ref.shape[0]
    u = u_ref[...]
    first = (i % seq_tiles) == 0
    last = (i % seq_tiles) == seq_tiles - 1
    prev_row = jnp.where(first, 0.0, up_ref[SUBLANES - 1:SUBLANES, :])
    next_row = jnp.where(last, 0.0, un_ref[0:1, :])
    row = lax.broadcasted_iota(jnp.int32, u.shape, 0)
    u_m = jnp.where(row == 0, prev_row, pltpu.roll(u, 1, 0))
    u_p = jnp.where(row == tm - 1, next_row, pltpu.roll(u, tm - 1, 0))
    y = cw_ref[0:1, :] * u_m + cw_ref[1:2, :] * u + cw_ref[2:3, :] * u_p
    z = (b_ref[...] * y).astype(BF16)
    o_ref[...] = x_ref[...] + jnp.dot(z, w_ref[...], preferred_element_type=F32)


def _conv_mixer(x2d, seq, g, w_in, conv_w, w_out):
    t, d = x2d.shape
    tm = _tile(seq, 512)
    nt = t // tm
    row = lambda i: (i, 0)
    fixed = lambda i: (0, 0)
    u, b = pl.pallas_call(
        _conv_in_kernel,
        grid=(nt,),
        in_specs=[pl.BlockSpec((tm, d), row), pl.BlockSpec((1, d), fixed),
                  pl.BlockSpec((d, 3 * d), fixed)],
        out_specs=[pl.BlockSpec((tm, d), row), pl.BlockSpec((tm, d), row)],
        out_shape=[jax.ShapeDtypeStruct((t, d), F32)] * 2,
        compiler_params=_params("parallel"),
        name="conv_in",
    )(x2d, g.reshape(1, d), w_in.astype(BF16))
    r8 = tm // SUBLANES
    n8 = t // SUBLANES
    return pl.pallas_call(
        functools.partial(_conv_out_kernel, seq // tm),
        grid=(nt,),
        in_specs=[pl.BlockSpec((tm, d), row),
                  pl.BlockSpec((SUBLANES, d), lambda i: (jnp.maximum(i * r8 - 1, 0), 0)),
                  pl.BlockSpec((SUBLANES, d), lambda i: (jnp.minimum((i + 1) * r8, n8 - 1), 0)),
                  pl.BlockSpec((tm, d), row), pl.BlockSpec((tm, d), row),
                  pl.BlockSpec((CONV_WIDTH, d), fixed), pl.BlockSpec((d, d), fixed)],
        out_specs=pl.BlockSpec((tm, d), row),
        out_shape=jax.ShapeDtypeStruct((t, d), F32),
        compiler_params=_params("parallel"),
        name="conv_out",
    )(u, u, u, b, x2d, conv_w, w_out.astype(BF16))


def _topk_rows(s, k):
    n = s.shape[0]
    iota = lax.broadcasted_iota(jnp.int32, s.shape, 0).astype(F32)
    vals, idxs = [], []
    for _ in range(k):
        m = jnp.max(s, axis=0, keepdims=True)
        idx = jnp.min(jnp.where(s == m, iota, float(n)), axis=0, keepdims=True)
        vals.append(m)
        idxs.append(idx)
        s = jnp.where(iota == idx, -jnp.inf, s)
    return vals, idxs


def _pair_topk(v1, i1, v2, i2, k, n_keys):
    g = SUBLANES
    w = lax.broadcasted_iota(jnp.int32, (g, v1[0].shape[1]), 0)
    wf = w.astype(F32)
    cat = lambda rows: jnp.concatenate(rows, axis=0)
    neg = -jnp.inf
    cands, poss, eids = [], [], []
    for r in range(k):
        cmax = k // (r + 1)
        if cmax < 4:
            break
        for c0 in range(0, cmax, g):
            val = v1[r] + cat(v2[c0:c0 + g])
            eid = i1[r] * float(n_keys) + cat(i2[c0:c0 + g])
            if c0 + g > cmax:
                val = jnp.where(w < cmax - c0, val, neg)
            cands.append(val)
            poss.append(wf + float(r * k + c0))
            eids.append(eid)
    r_done = r
    for c in range(k):
        rmax = k // (c + 1)
        if rmax <= r_done:
            break
        for r0 in range((r_done // g) * g, rmax, g):
            val = cat(v1[r0:r0 + g]) + v2[c]
            eid = cat(i1[r0:r0 + g]) * float(n_keys) + i2[c]
            lo, hi = max(r_done - r0, 0), min(rmax - r0, g)
            if lo > 0 or hi < g:
                val = jnp.where((w >= lo) & (w < hi), val, neg)
            cands.append(val)
            poss.append(wf * float(k) + float(r0 * k + c))
            eids.append(eid)
    cand, pos, eid = cat(cands), cat(poss), cat(eids)
    tops, sel_e = [], []
    for _ in range(k):
        m = jnp.max(cand, axis=0, keepdims=True)
        p = jnp.min(jnp.where(cand == m, pos, float(k * k)), axis=0, keepdims=True)
        hit = pos == p
        tops.append(m)
        sel_e.append(jnp.max(jnp.where(hit, eid, -1.0), axis=0, keepdims=True))
        cand = jnp.where(hit, neg, cand)
    return tops, sel_e


def _route_kernel(x_ref, g_ref, wq_ref, k1_ref, k2_ref, xn_ref, e_ref, gate_ref, xn_s, q_s):
    h = pl.program_id(1)
    tm = x_ref.shape[0]
    nsub = tm // LANES

    @pl.when(h == 0)
    def _():
        xn = _rms(x_ref[...], g_ref[...]).astype(BF16)
        xn_s[...] = xn
        xn_ref[...] = xn

    q = jnp.dot(xn_s[...], wq_ref[...], preferred_element_type=F32)
    q_s[...] = q.astype(BF16).reshape(nsub, LANES, 2 * PEER_D_HALF)
    k1 = k1_ref[0]
    k2 = k2_ref[0]

    def sub_block(c, carry):
        qc = q_s[c]
        s1 = lax.dot_general(k1, qc[:, :PEER_D_HALF], _NT, preferred_element_type=F32)
        s2 = lax.dot_general(k2, qc[:, PEER_D_HALF:], _NT, preferred_element_type=F32)
        v1, i1 = _topk_rows(s1, PEER_TOPK)
        v2, i2 = _topk_rows(s2, PEER_TOPK)
        tops, eids = _pair_topk(v1, i1, v2, i2, PEER_TOPK, PEER_N_KEYS)
        top = jnp.concatenate(tops, axis=0)
        ex = jnp.exp(top - tops[0])
        gate_ref[0, c] = ex / jnp.sum(ex, axis=0, keepdims=True)
        e_ref[0, c] = jnp.concatenate(eids, axis=0).astype(jnp.int32)
        return carry

    lax.fori_loop(0, nsub, sub_block, 0)


def _peer_route(x2d, g, w_q, k1, k2):
    t, d = x2d.shape
    tm = _tile(t, 512)
    nsub = tm // LANES
    hq = 2 * PEER_D_HALF
    xn, eid, gate = pl.pallas_call(
        _route_kernel,
        grid=(t // tm, PEER_HEADS),
        in_specs=[pl.BlockSpec((tm, d), lambda i, h: (i, 0)),
                  pl.BlockSpec((1, d), lambda i, h: (0, 0)),
                  pl.BlockSpec((d, hq), lambda i, h: (0, h)),
                  pl.BlockSpec((1, PEER_N_KEYS, PEER_D_HALF), lambda i, h: (h, 0, 0)),
                  pl.BlockSpec((1, PEER_N_KEYS, PEER_D_HALF), lambda i, h: (h, 0, 0))],
        out_specs=[pl.BlockSpec((tm, d), lambda i, h: (i, 0)),
                   pl.BlockSpec((1, nsub, PEER_TOPK, LANES), lambda i, h: (h, i, 0, 0)),
                   pl.BlockSpec((1, nsub, PEER_TOPK, LANES), lambda i, h: (h, i, 0, 0))],
        out_shape=[jax.ShapeDtypeStruct((t, d), BF16),
                   jax.ShapeDtypeStruct((PEER_HEADS, t // LANES, PEER_TOPK, LANES), jnp.int32),
                   jax.ShapeDtypeStruct((PEER_HEADS, t // LANES, PEER_TOPK, LANES), F32)],
        scratch_shapes=[pltpu.VMEM((tm, d), BF16), pltpu.VMEM((nsub, LANES, hq), BF16)],
        compiler_params=_params("parallel", "arbitrary"),
        name="peer_route",
    )(x2d, g.reshape(1, d), w_q.astype(BF16), k1.astype(BF16), k2.astype(BF16))
    return xn, eid, gate


def _gate_matrix_kernel(e_ref, g_ref, o_ref):
    half = e_ref.shape[0]
    nk = PEER_N_KEYS
    lane = lax.broadcasted_iota(jnp.int32, (1, 2 * nk), 1)
    second = jnp.where(lane >= nk, nk, 0)
    sub1 = lax.broadcasted_iota(jnp.int32, (nk, 2 * nk), 0)
    sub2 = lax.broadcasted_iota(jnp.int32, (2 * nk, 2 * nk), 0)

    def pair(p, carry):
        e = e_ref[pl.ds(p, 1), :]
        g = g_ref[pl.ds(p, 1), :]
        i1 = e >> 7
        i2 = (e & (nk - 1)) + second
        lhs = jnp.where(sub1 == i1, g, 0.0).astype(BF16)
        rhs = jnp.where(sub2 == i2, 1.0, 0.0).astype(BF16)
        res = lax.dot_general(lhs, rhs, _NT, preferred_element_type=F32)
        o_ref[p] = res[:, :nk].astype(o_ref.dtype)
        o_ref[p + half] = res[:, nk:].astype(o_ref.dtype)
        return carry

    lax.fori_loop(0, half, pair, 0)


def _gate_matrix(eid, gate):
    hh, nb, kk, ln = eid.shape
    t = nb * ln
    nsel = hh * kk
    tb = _tile(t, 128)
    half = tb // 2

    def pairs(a):
        a = a.transpose(1, 3, 0, 2).reshape(t // tb, 2, half, nsel)
        return a.transpose(0, 2, 1, 3).reshape(t // 2, 2 * nsel)

    w3 = pl.pallas_call(
        _gate_matrix_kernel,
        grid=(t // tb,),
        in_specs=[pl.BlockSpec((half, 2 * nsel), lambda i: (i, 0)),
                  pl.BlockSpec((half, 2 * nsel), lambda i: (i, 0))],
        out_specs=pl.BlockSpec((tb, PEER_N_KEYS, PEER_N_KEYS), lambda i: (i, 0, 0)),
        out_shape=jax.ShapeDtypeStruct((t, PEER_N_KEYS, PEER_N_KEYS), BF16),
        compiler_params=_params("parallel"),
        name="peer_gate_matrix",
    )(pairs(eid), pairs(gate))
    return w3.reshape(t, PEER_N_KEYS * PEER_N_KEYS)


def _experts_kernel(final, xn_ref, ut_ref, v_ref, w_ref, x_ref, fg_ref, o_ref, acc_ref):
    j = pl.program_id(1)

    @pl.when(j == 0)
    def _():
        acc_ref[...] = x_ref[...]

    a = jnp.dot(xn_ref[...], ut_ref[...], preferred_element_type=F32)
    gelu = 0.5 * a * (1.0 + lax.erf(a * INV_SQRT2))
    hid = (gelu * w_ref[...].astype(F32)).astype(BF16)
    acc_ref[...] += jnp.dot(hid, v_ref[...], preferred_element_type=F32)

    @pl.when(j == pl.num_programs(1) - 1)
    def _():
        y = acc_ref[...]
        o_ref[...] = _rms(y, fg_ref[...]) if final else y


def _peer_experts(x2d, xn, w, u_t, v, final_gain):
    t, d = x2d.shape
    ne = v.shape[0]
    tm = _tile(t, 512)
    te = _tile(ne, 1024)
    final = final_gain is not None
    fg = (final_gain if final else jnp.ones((d,), F32)).reshape(1, d)
    return pl.pallas_call(
        functools.partial(_experts_kernel, final),
        grid=(t // tm, ne // te),
        in_specs=[pl.BlockSpec((tm, d), lambda i, j: (i, 0)),
                  pl.BlockSpec((d, te), lambda i, j: (0, j)),
                  pl.BlockSpec((te, d), lambda i, j: (j, 0)),
                  pl.BlockSpec((tm, te), lambda i, j: (i, j)),
                  pl.BlockSpec((tm, d), lambda i, j: (i, 0)),
                  pl.BlockSpec((1, d), lambda i, j: (0, 0))],
        out_specs=pl.BlockSpec((tm, d), lambda i, j: (i, 0)),
        out_shape=jax.ShapeDtypeStruct((t, d), F32),
        scratch_shapes=[pltpu.VMEM((tm, d), F32)],
        compiler_params=_params("parallel", "arbitrary"),
        name="peer_experts",
    )(xn, u_t, v, w, x2d, fg)


def _peer_ffn(x2d, g, w_q, k1, k2, u_tab, v_tab, final_gain=None):
    xn, eid, gate = _peer_route(x2d, g, w_q, k1, k2)
    w = _gate_matrix(eid, gate)
    return _peer_experts(x2d, xn, w, u_tab.astype(BF16).T, v_tab.astype(BF16), final_gain)


def _mla_proj_kernel(scale, x_ref, g_ref, win_ref, qg_ref, kvg_ref, wqm_ref, wqr_ref, wk_ref, wv_ref,
                     cos_ref, sin_ref, q_ref, k_ref, v_ref):
    xn = _rms(x_ref[...], g_ref[...]).astype(BF16)
    lat = jnp.dot(xn, win_ref[...], preferred_element_type=F32)
    nq, nkv = MLA_Q_LORA, MLA_KV_LORA
    c_q = _rms(lat[:, :nq], qg_ref[...]).astype(BF16)
    c_kv = _rms(lat[:, nq:nq + nkv], kvg_ref[...]).astype(BF16)
    cos = cos_ref[...]
    sin = sin_ref[...]
    k_rope = lat[:, nq + nkv:nq + nkv + HEAD_SLAB] * cos + lat[:, nq + nkv + HEAD_SLAB:] * sin
    qm = jnp.dot(c_q, wqm_ref[...], preferred_element_type=F32)
    qr = jnp.dot(c_q, wqr_ref[...], preferred_element_type=F32)
    kn = jnp.dot(c_kv, wk_ref[...], preferred_element_type=F32)
    v_ref[...] = jnp.dot(c_kv, wv_ref[...], preferred_element_type=F32).astype(v_ref.dtype)
    for h in range(MLA_HEADS):
        sl = slice(h * HEAD_SLAB, (h + 1) * HEAD_SLAB)
        q_ref[:, sl] = ((qm[:, sl] * cos + qr[:, sl] * sin) * scale).astype(q_ref.dtype)
        k_ref[:, sl] = (kn[:, sl] + k_rope).astype(k_ref.dtype)


def _attn_kernel(q_ref, k_ref, v_ref, o_ref):
    v2 = v_ref[...]
    outs = []
    for hh in range(2):
        sl = slice(hh * HEAD_SLAB, (hh + 1) * HEAD_SLAB)
        s = lax.dot_general(q_ref[:, sl], k_ref[:, sl], _NT, preferred_element_type=F32)
        m = jnp.max(s, axis=-1, keepdims=True)
        p = jnp.exp(s - m)
        l = jnp.sum(p, axis=-1, keepdims=True)
        pv = jnp.dot(p.astype(BF16), v2, preferred_element_type=F32)
        outs.append(pv / l)
    lane = lax.broadcasted_iota(jnp.int32, outs[0].shape, 1)
    o_ref[...] = jnp.where(lane < MLA_V, outs[0], outs[1]).astype(o_ref.dtype)


def _out_proj_kernel(o_ref, w_ref, x_ref, y_ref):
    y_ref[...] = x_ref[...] + jnp.dot(o_ref[...], w_ref[...], preferred_element_type=F32)


def _rot_half(w):
    hr = MLA_ROPE // 2
    return jnp.concatenate([-w[..., hr:], w[..., :hr]], axis=-1)


def _mla_mixer(x2d, bsz, seq, g, w_in, q_norm, kv_norm, w_uq, w_ukv, w_o):
    t, d = x2d.shape
    nh, nq, nkv = MLA_HEADS, MLA_Q_LORA, MLA_KV_LORA
    pad = HEAD_SLAB - MLA_NOPE - MLA_ROPE

    w_kr = w_in[:, nq + nkv:]
    zl = jnp.zeros((d, MLA_NOPE), F32)
    zr = jnp.zeros((d, pad), F32)
    win_ext = jnp.concatenate([w_in[:, :nq + nkv], zl, w_kr, zr, zl, _rot_half(w_kr), zr], axis=1)
    wq = w_uq.reshape(nq, nh, MLA_NOPE + MLA_ROPE)
    zq = jnp.zeros((nq, nh, pad), F32)
    wq_main = jnp.concatenate([wq, zq], axis=-1).reshape(nq, nh * HEAD_SLAB)
    wq_rot = jnp.concatenate([jnp.zeros((nq, nh, MLA_NOPE), F32), _rot_half(wq[..., MLA_NOPE:]), zq],
                             axis=-1).reshape(nq, nh * HEAD_SLAB)
    wkv = w_ukv.reshape(nkv, nh, MLA_NOPE + MLA_V)
    wk = jnp.concatenate([wkv[..., :MLA_NOPE], jnp.zeros((nkv, nh, HEAD_SLAB - MLA_NOPE), F32)],
                         axis=-1).reshape(nkv, nh * HEAD_SLAB)
    wv = wkv[..., MLA_NOPE:].reshape(nkv, nh * MLA_V)

    pos = jnp.arange(seq, dtype=F32)
    inv_freq = ROPE_THETA ** (-jnp.arange(0, MLA_ROPE, 2, dtype=F32) / MLA_ROPE)
    ang = pos[:, None] * inv_freq[None, :]
    cos, sin = jnp.cos(ang), jnp.sin(ang)
    cos_t = jnp.concatenate([jnp.ones((seq, MLA_NOPE), F32), cos, cos, jnp.zeros((seq, pad), F32)], axis=1)
    sin_t = jnp.concatenate([jnp.zeros((seq, MLA_NOPE), F32), sin, sin, jnp.zeros((seq, pad), F32)], axis=1)

    tm = _tile(seq, 512)
    spt = seq // tm
    row = lambda i: (i, 0)
    fixed = lambda i: (0, 0)
    nlat = win_ext.shape[1]
    q, k, v = pl.pallas_call(
        functools.partial(_mla_proj_kernel, float((MLA_NOPE + MLA_ROPE) ** -0.5)),
        grid=(t // tm,),
        in_specs=[pl.BlockSpec((tm, d), row), pl.BlockSpec((1, d), fixed),
                  pl.BlockSpec((d, nlat), fixed),
                  pl.BlockSpec((1, nq), fixed), pl.BlockSpec((1, nkv), fixed),
                  pl.BlockSpec((nq, nh * HEAD_SLAB), fixed), pl.BlockSpec((nq, nh * HEAD_SLAB), fixed),
                  pl.BlockSpec((nkv, nh * HEAD_SLAB), fixed), pl.BlockSpec((nkv, nh * MLA_V), fixed),
                  pl.BlockSpec((tm, HEAD_SLAB), lambda i: (i % spt, 0)),
                  pl.BlockSpec((tm, HEAD_SLAB), lambda i: (i % spt, 0))],
        out_specs=[pl.BlockSpec((tm, nh * HEAD_SLAB), row), pl.BlockSpec((tm, nh * HEAD_SLAB), row),
                   pl.BlockSpec((tm, nh * MLA_V), row)],
        out_shape=[jax.ShapeDtypeStruct((t, nh * HEAD_SLAB), BF16),
                   jax.ShapeDtypeStruct((t, nh * HEAD_SLAB), BF16),
                   jax.ShapeDtypeStruct((t, nh * MLA_V), BF16)],
        compiler_params=_params("parallel"),
        name="mla_proj",
    )(x2d, g.reshape(1, d), win_ext.astype(BF16), q_norm.reshape(1, nq), kv_norm.reshape(1, nkv),
      wq_main.astype(BF16), wq_rot.astype(BF16), wk.astype(BF16), wv.astype(BF16), cos_t, sin_t)

    tq = _tile(seq, 512)
    qpt = seq // tq
    o = pl.pallas_call(
        _attn_kernel,
        grid=(bsz, nh // 2, qpt),
        in_specs=[pl.BlockSpec((tq, 2 * HEAD_SLAB), lambda b, hp, qi: (b * qpt + qi, hp)),
                  pl.BlockSpec((seq, 2 * HEAD_SLAB), lambda b, hp, qi: (b, hp)),
                  pl.BlockSpec((seq, 2 * MLA_V), lambda b, hp, qi: (b, hp))],
        out_specs=pl.BlockSpec((tq, 2 * MLA_V), lambda b, hp, qi: (b * qpt + qi, hp)),
        out_shape=jax.ShapeDtypeStruct((t, nh * MLA_V), BF16),
        compiler_params=_params("parallel", "parallel", "arbitrary"),
        name="mla_attention",
    )(q, k, v)

    tmo = _tile(t, 1024)
    return pl.pallas_call(
        _out_proj_kernel,
        grid=(t // tmo,),
        in_specs=[pl.BlockSpec((tmo, nh * MLA_V), row), pl.BlockSpec((nh * MLA_V, d), fixed),
                  pl.BlockSpec((tmo, d), row)],
        out_specs=pl.BlockSpec((tmo, d), row),
        out_shape=jax.ShapeDtypeStruct((t, d), F32),
        compiler_params=_params("parallel"),
        name="mla_out_proj",
    )(o, w_o.astype(BF16), x2d)


def kernel(x, norm_mix, norm_ffn, conv_w_in, conv_w, conv_w_out, mla_w_in, mla_q_norm, mla_kv_norm,
           mla_w_uq, mla_w_ukv, mla_w_o, peer_w_q, peer_k1, peer_k2, peer_u, peer_v, final_norm):
    bsz, seq, d = x.shape
    depth = norm_mix.shape[0]
    h = x.reshape(bsz * seq, d)
    for i in range(depth):
        j = i // 2
        if i % 2 == 0:
            h = _conv_mixer(h, seq, norm_mix[i], conv_w_in[j], conv_w[j], conv_w_out[j])
        else:
            h = _mla_mixer(h, bsz, seq, norm_mix[i], mla_w_in[j], mla_q_norm[j], mla_kv_norm[j],
                           mla_w_uq[j], mla_w_ukv[j], mla_w_o[j])
        h = _peer_ffn(h, norm_ffn[i], peer_w_q[i], peer_k1[i], peer_k2[i], peer_u[i], peer_v[i],
                      final_norm if i == depth - 1 else None)
    return h.reshape(bsz, seq, d)
```

```python
import functools

import jax
import jax.numpy as jnp
from jax import lax
from jax.experimental import pallas as pl
from jax.experimental.pallas import tpu as pltpu

F32 = jnp.float32
BF16 = jnp.bfloat16

NORM_EPS = 1e-6
CONV_WIDTH = 3
MLA_HEADS = 16
MLA_Q_LORA = 384
MLA_KV_LORA = 256
MLA_NOPE = 64
MLA_ROPE = 32
MLA_V = 64
ROPE_THETA = 10000.0
PEER_HEADS = 8
PEER_N_KEYS = 128
PEER_TOPK = 16
PEER_D_HALF = 128

LANES = 128
SUBLANES = 8
HEAD_SLAB = 128
VMEM_LIMIT = 56 * 1024 * 1024
GATE_UNROLL = 16

INV_SQRT2 = 0.7071067811865476
_NT = (((1,), (1,)), ((), ()))


def _tile(n, pref):
    t = min(n, pref)
    while n % t:
        t -= 1
    return t


def _params(*sem):
    return pltpu.CompilerParams(dimension_semantics=sem, vmem_limit_bytes=VMEM_LIMIT)


def _rms(x, g):
    return x * lax.rsqrt(jnp.mean(x * x, axis=-1, keepdims=True) + NORM_EPS) * g


def _conv_in_kernel(x_ref, g_ref, w_ref, u_ref, b_ref):
    d = x_ref.shape[1]
    xn = _rms(x_ref[...], g_ref[...]).astype(BF16)
    b_ref[...] = jnp.dot(xn, w_ref[:, :d], preferred_element_type=F32)
    c = jnp.dot(xn, w_ref[:, d:2 * d], preferred_element_type=F32)
    h = jnp.dot(xn, w_ref[:, 2 * d:], preferred_element_type=F32)
    u_ref[...] = c * h


def _conv_out_kernel(seq_tiles, u_ref, up_ref, un_ref, b_ref, x_ref, cw_ref, w_ref, o_ref):
    i = pl.program_id(0)
    tm = u_ref.shape[0]
    u = u_ref[...]
    first = (i % seq_tiles) == 0
    last = (i % seq_tiles) == seq_tiles - 1
    prev_row = jnp.where(first, 0.0, up_ref[SUBLANES - 1:SUBLANES, :])
    next_row = jnp.where(last, 0.0, un_ref[0:1, :])
    row = lax.broadcasted_iota(jnp.int32, u.shape, 0)
    u_m = jnp.where(row == 0, prev_row, pltpu.roll(u, 1, 0))
    u_p = jnp.where(row == tm - 1, next_row, pltpu.roll(u, tm - 1, 0))
    y = cw_ref[0:1, :] * u_m + cw_ref[1:2, :] * u + cw_ref[2:3, :] * u_p
    z = (b_ref[...] * y).astype(BF16)
    o_ref[...] = x_ref[...] + jnp.dot(z, w_ref[...], preferred_element_type=F32)


def _conv_mixer(x2d, seq, g, w_in, conv_w, w_out):
    t, d = x2d.shape
    tm = _tile(seq, 512)
    nt = t // tm
    row = lambda i: (i, 0)
    fixed = lambda i: (0, 0)
    u, b = pl.pallas_call(
        _conv_in_kernel,
        grid=(nt,),
        in_specs=[pl.BlockSpec((tm, d), row), pl.BlockSpec((1, d), fixed),
                  pl.BlockSpec((d, 3 * d), fixed)],
        out_specs=[pl.BlockSpec((tm, d), row), pl.BlockSpec((tm, d), row)],
        out_shape=[jax.ShapeDtypeStruct((t, d), F32)] * 2,
        compiler_params=_params("parallel"),
        name="conv_in",
    )(x2d, g.reshape(1, d), w_in.astype(BF16))
    r8 = tm // SUBLANES
    n8 = t // SUBLANES
    return pl.pallas_call(
        functools.partial(_conv_out_kernel, seq // tm),
        grid=(nt,),
        in_specs=[pl.BlockSpec((tm, d), row),
                  pl.BlockSpec((SUBLANES, d), lambda i: (jnp.maximum(i * r8 - 1, 0), 0)),
                  pl.BlockSpec((SUBLANES, d), lambda i: (jnp.minimum((i + 1) * r8, n8 - 1), 0)),
                  pl.BlockSpec((tm, d), row), pl.BlockSpec((tm, d), row),
                  pl.BlockSpec((CONV_WIDTH, d), fixed), pl.BlockSpec((d, d), fixed)],
        out_specs=pl.BlockSpec((tm, d), row),
        out_shape=jax.ShapeDtypeStruct((t, d), F32),
        compiler_params=_params("parallel"),
        name="conv_out",
    )(u, u, u, b, x2d, conv_w, w_out.astype(BF16))


def _topk_rows(s, k):
    n = s.shape[0]
    iota = lax.broadcasted_iota(jnp.int32, s.shape, 0).astype(F32)
    vals, idxs = [], []
    for _ in range(k):
        m = jnp.max(s, axis=0, keepdims=True)
        idx = jnp.min(jnp.where(s == m, iota, float(n)), axis=0, keepdims=True)
        vals.append(m)
        idxs.append(idx)
        s = jnp.where(iota == idx, -jnp.inf, s)
    return vals, idxs


def _pair_topk(v1, i1, v2, i2, k, n_keys):
    g = SUBLANES
    w = lax.broadcasted_iota(jnp.int32, (g, v1[0].shape[1]), 0)
    wf = w.astype(F32)
    cat = lambda rows: jnp.concatenate(rows, axis=0)
    neg = -jnp.inf
    cands, poss, eids = [], [], []
    for r in range(k):
        cmax = k // (r + 1)
        if cmax < 4:
            break
        for c0 in range(0, cmax, g):
            val = v1[r] + cat(v2[c0:c0 + g])
            eid = i1[r] * float(n_keys) + cat(i2[c0:c0 + g])
            if c0 + g > cmax:
                val = jnp.where(w < cmax - c0, val, neg)
            cands.append(val)
            poss.append(wf + float(r * k + c0))
            eids.append(eid)
    r_done = r
    for c in range(k):
        rmax = k // (c + 1)
        if rmax <= r_done:
            break
        for r0 in range((r_done // g) * g, rmax, g):
            val = cat(v1[r0:r0 + g]) + v2[c]
            eid = cat(i1[r0:r0 + g]) * float(n_keys) + i2[c]
            lo, hi = max(r_done - r0, 0), min(rmax - r0, g)
            if lo > 0 or hi < g:
                val = jnp.where((w >= lo) & (w < hi), val, neg)
            cands.append(val)
            poss.append(wf * float(k) + float(r0 * k + c))
            eids.append(eid)
    cand, pos, eid = cat(cands), cat(poss), cat(eids)
    tops, sel_e = [], []
    for _ in range(k):
        m = jnp.max(cand, axis=0, keepdims=True)
        p = jnp.min(jnp.where(cand == m, pos, float(k * k)), axis=0, keepdims=True)
        hit = pos == p
        tops.append(m)
        sel_e.append(jnp.max(jnp.where(hit, eid, -1.0), axis=0, keepdims=True))
        cand = jnp.where(hit, neg, cand)
    return tops, sel_e


def _route_kernel(x_ref, g_ref, wq_ref, k1_ref, k2_ref, xn_ref, e_ref, gate_ref, xn_s, q_s):
    h = pl.program_id(1)
    tm = x_ref.shape[0]
    nsub = tm // LANES

    @pl.when(h == 0)
    def _():
        xn = _rms(x_ref[...], g_ref[...]).astype(BF16)
        xn_s[...] = xn
        xn_ref[...] = xn

    q = jnp.dot(xn_s[...], wq_ref[...], preferred_element_type=F32)
    q_s[...] = q.astype(BF16).reshape(nsub, LANES, 2 * PEER_D_HALF)
    k1 = k1_ref[0]
    k2 = k2_ref[0]

    def sub_block(c, carry):
        qc = q_s[c]
        s1 = lax.dot_general(k1, qc[:, :PEER_D_HALF], _NT, preferred_element_type=F32)
        s2 = lax.dot_general(k2, qc[:, PEER_D_HALF:], _NT, preferred_element_type=F32)
        v1, i1 = _topk_rows(s1, PEER_TOPK)
        v2, i2 = _topk_rows(s2, PEER_TOPK)
        tops, eids = _pair_topk(v1, i1, v2, i2, PEER_TOPK, PEER_N_KEYS)
        top = jnp.concatenate(tops, axis=0)
        ex = jnp.exp(top - tops[0])
        gate_ref[0, c] = ex / jnp.sum(ex, axis=0, keepdims=True)
        e_ref[0, c] = jnp.concatenate(eids, axis=0).astype(jnp.int32)
        return carry

    lax.fori_loop(0, nsub, sub_block, 0)


def _peer_route(x2d, g, w_q, k1, k2):
    t, d = x2d.shape
    tm = _tile(t, 512)
    nsub = tm // LANES
    hq = 2 * PEER_D_HALF
    xn, eid, gate = pl.pallas_call(
        _route_kernel,
        grid=(t // tm, PEER_HEADS),
        in_specs=[pl.BlockSpec((tm, d), lambda i, h: (i, 0)),
                  pl.BlockSpec((1, d), lambda i, h: (0, 0)),
                  pl.BlockSpec((d, hq), lambda i, h: (0, h)),
                  pl.BlockSpec((1, PEER_N_KEYS, PEER_D_HALF), lambda i, h: (h, 0, 0)),
                  pl.BlockSpec((1, PEER_N_KEYS, PEER_D_HALF), lambda i, h: (h, 0, 0))],
        out_specs=[pl.BlockSpec((tm, d), lambda i, h: (i, 0)),
                   pl.BlockSpec((1, nsub, PEER_TOPK, LANES), lambda i, h: (h, i, 0, 0)),
                   pl.BlockSpec((1, nsub, PEER_TOPK, LANES), lambda i, h: (h, i, 0, 0))],
        out_shape=[jax.ShapeDtypeStruct((t, d), BF16),
                   jax.ShapeDtypeStruct((PEER_HEADS, t // LANES, PEER_TOPK, LANES), jnp.int32),
                   jax.ShapeDtypeStruct((PEER_HEADS, t // LANES, PEER_TOPK, LANES), F32)],
        scratch_shapes=[pltpu.VMEM((tm, d), BF16), pltpu.VMEM((nsub, LANES, hq), BF16)],
        compiler_params=_params("parallel", "arbitrary"),
        name="peer_route",
    )(x2d, g.reshape(1, d), w_q.astype(BF16), k1.astype(BF16), k2.astype(BF16))
    return xn, eid, gate


def _pack_bf16_pair(a, b):
    abits = lax.bitcast_convert_type(a.astype(BF16).astype(F32), jnp.uint32)
    bbits = lax.bitcast_convert_type(b.astype(BF16).astype(F32), jnp.uint32)
    return (abits >> 16) | (bbits & jnp.uint32(0xFFFF0000))


def _unpack_bf16_pair(w):
    lo = lax.bitcast_convert_type(w << 16, F32)
    hi = lax.bitcast_convert_type(w & jnp.uint32(0xFFFF0000), F32)
    return lo, hi


def _gate_matrix_kernel(e_ref, g_ref, o_ref):
    pb = e_ref.shape[0]
    nk = PEER_N_KEYS
    lane = lax.broadcasted_iota(jnp.int32, (1, 2 * nk), 1)
    second = jnp.where(lane >= nk, nk, 0)
    sub1 = lax.broadcasted_iota(jnp.int32, (nk, 2 * nk), 0)
    sub2 = lax.broadcasted_iota(jnp.int32, (2 * nk, 2 * nk), 0)

    def pair(p, carry):
        e = e_ref[pl.ds(p, 1), :]
        g = g_ref[pl.ds(p, 1), :]
        i1 = e >> 7
        i2 = (e & (nk - 1)) + second
        lhs = jnp.where(sub1 == i1, g, 0.0).astype(BF16)
        rhs = jnp.where(sub2 == i2, 1.0, 0.0).astype(BF16)
        res = lax.dot_general(lhs, rhs, _NT, preferred_element_type=F32)
        packed = _pack_bf16_pair(res[:, :nk], res[:, nk:])
        for c in range(nk // SUBLANES):
            o_ref[c, p] = packed[c * SUBLANES:(c + 1) * SUBLANES, :]
        return carry

    lax.fori_loop(0, pb, pair, 0, unroll=GATE_UNROLL)


def _gate_matrix(eid, gate, tm):
    hh, nb, kk, ln = eid.shape
    t = nb * ln
    nsel = hh * kk
    half = tm // 2
    nchunk = PEER_N_KEYS // SUBLANES
    pb = _tile(half, 64)

    def pairs(a):
        a = a.transpose(1, 3, 0, 2).reshape(t // tm, 2, half, nsel)
        return a.transpose(0, 2, 1, 3).reshape(t // 2, 2 * nsel)

    spt = half // pb
    wp = pl.pallas_call(
        _gate_matrix_kernel,
        grid=(t // tm, spt),
        in_specs=[pl.BlockSpec((pb, 2 * nsel), lambda i, s: (i * spt + s, 0)),
                  pl.BlockSpec((pb, 2 * nsel), lambda i, s: (i * spt + s, 0))],
        out_specs=pl.BlockSpec((nchunk, pb, SUBLANES, PEER_N_KEYS), lambda i, s: (i, s, 0, 0)),
        out_shape=jax.ShapeDtypeStruct((t // tm * nchunk, half, SUBLANES, PEER_N_KEYS), jnp.uint32),
        compiler_params=_params("parallel", "parallel"),
        name="peer_gate_matrix",
    )(pairs(eid), pairs(gate))
    return wp.reshape(t // tm * nchunk * half * SUBLANES, PEER_N_KEYS)


def _experts_kernel(final, xn_ref, ut_ref, v_ref, w_ref, x_ref, fg_ref, o_ref, acc_ref):
    j = pl.program_id(1)
    half = x_ref.shape[0] // 2

    @pl.when(j == 0)
    def _():
        acc_ref[...] = x_ref[...]

    a = jnp.dot(xn_ref[...], ut_ref[...], preferred_element_type=F32)
    cols = []
    for c in range(SUBLANES):
        wc = w_ref[pl.ds(c, half, stride=SUBLANES), :]
        cols.append(jnp.concatenate(_unpack_bf16_pair(wc), axis=0))
    w = jnp.concatenate(cols, axis=1)
    gelu = 0.5 * a * (1.0 + lax.erf(a * INV_SQRT2))
    hid = (gelu * w).astype(BF16)
    acc_ref[...] += jnp.dot(hid, v_ref[...], preferred_element_type=F32)

    @pl.when(j == pl.num_programs(1) - 1)
    def _():
        y = acc_ref[...]
        o_ref[...] = _rms(y, fg_ref[...]) if final else y


def _peer_experts(x2d, xn, w, u_t, v, final_gain, tm):
    t, d = x2d.shape
    ne = v.shape[0]
    te = SUBLANES * PEER_N_KEYS
    nj = ne // te
    final = final_gain is not None
    fg = (final_gain if final else jnp.ones((d,), F32)).reshape(1, d)
    return pl.pallas_call(
        functools.partial(_experts_kernel, final),
        grid=(t // tm, nj),
        in_specs=[pl.BlockSpec((tm, d), lambda i, j: (i, 0)),
                  pl.BlockSpec((d, te), lambda i, j: (0, j)),
                  pl.BlockSpec((te, d), lambda i, j: (j, 0)),
                  pl.BlockSpec((tm // 2 * SUBLANES, PEER_N_KEYS), lambda i, j: (i * nj + j, 0)),
                  pl.BlockSpec((tm, d), lambda i, j: (i, 0)),
                  pl.BlockSpec((1, d), lambda i, j: (0, 0))],
        out_specs=pl.BlockSpec((tm, d), lambda i, j: (i, 0)),
        out_shape=jax.ShapeDtypeStruct((t, d), F32),
        scratch_shapes=[pltpu.VMEM((tm, d), F32)],
        compiler_params=_params("parallel", "arbitrary"),
        name="peer_experts",
    )(xn, u_t, v, w, x2d, fg)


def _peer_ffn(x2d, g, w_q, k1, k2, u_tab, v_tab, final_gain=None):
    tm = _tile(x2d.shape[0], 512)
    xn, eid, gate = _peer_route(x2d, g, w_q, k1, k2)
    w = _gate_matrix(eid, gate, tm)
    return _peer_experts(x2d, xn, w, u_tab.astype(BF16).T, v_tab.astype(BF16), final_gain, tm)


def _mla_proj_kernel(scale, x_ref, g_ref, win_ref, qg_ref, kvg_ref, wqm_ref, wqr_ref, wk_ref, wv_ref,
                     cos_ref, sin_ref, q_ref, k_ref, v_ref):
    xn = _rms(x_ref[...], g_ref[...]).astype(BF16)
    lat = jnp.dot(xn, win_ref[...], preferred_element_type=F32)
    nq, nkv = MLA_Q_LORA, MLA_KV_LORA
    c_q = _rms(lat[:, :nq], qg_ref[...]).astype(BF16)
    c_kv = _rms(lat[:, nq:nq + nkv], kvg_ref[...]).astype(BF16)
    cos = cos_ref[...]
    sin = sin_ref[...]
    k_rope = lat[:, nq + nkv:nq + nkv + HEAD_SLAB] * cos + lat[:, nq + nkv + HEAD_SLAB:] * sin
    qm = jnp.dot(c_q, wqm_ref[...], preferred_element_type=F32)
    qr = jnp.dot(c_q, wqr_ref[...], preferred_element_type=F32)
    kn = jnp.dot(c_kv, wk_ref[...], preferred_element_type=F32)
    v_ref[...] = jnp.dot(c_kv, wv_ref[...], preferred_element_type=F32).astype(v_ref.dtype)
    for h in range(MLA_HEADS):
        sl = slice(h * HEAD_SLAB, (h + 1) * HEAD_SLAB)
        q_ref[:, sl] = ((qm[:, sl] * cos + qr[:, sl] * sin) * scale).astype(q_ref.dtype)
        k_ref[:, sl] = (kn[:, sl] + k_rope).astype(k_ref.dtype)


def _attn_kernel(q_ref, k_ref, v_ref, o_ref):
    v2 = v_ref[...]
    outs = []
    for hh in range(2):
        sl = slice(hh * HEAD_SLAB, (hh + 1) * HEAD_SLAB)
        s = lax.dot_general(q_ref[:, sl], k_ref[:, sl], _NT, preferred_element_type=F32)
        m = jnp.max(s, axis=-1, keepdims=True)
        p = jnp.exp(s - m)
        l = jnp.sum(p, axis=-1, keepdims=True)
        pv = jnp.dot(p.astype(BF16), v2, preferred_element_type=F32)
        outs.append(pv / l)
    lane = lax.broadcasted_iota(jnp.int32, outs[0].shape, 1)
    o_ref[...] = jnp.where(lane < MLA_V, outs[0], outs[1]).astype(o_ref.dtype)


def _out_proj_kernel(o_ref, w_ref, x_ref, y_ref):
    y_ref[...] = x_ref[...] + jnp.dot(o_ref[...], w_ref[...], preferred_element_type=F32)


def _rot_half(w):
    hr = MLA_ROPE // 2
    return jnp.concatenate([-w[..., hr:], w[..., :hr]], axis=-1)


def _mla_mixer(x2d, bsz, seq, g, w_in, q_norm, kv_norm, w_uq, w_ukv, w_o):
    t, d = x2d.shape
    nh, nq, nkv = MLA_HEADS, MLA_Q_LORA, MLA_KV_LORA
    pad = HEAD_SLAB - MLA_NOPE - MLA_ROPE

    w_kr = w_in[:, nq + nkv:]
    zl = jnp.zeros((d, MLA_NOPE), F32)
    zr = jnp.zeros((d, pad), F32)
    win_ext = jnp.concatenate([w_in[:, :nq + nkv], zl, w_kr, zr, zl, _rot_half(w_kr), zr], axis=1)
    wq = w_uq.reshape(nq, nh, MLA_NOPE + MLA_ROPE)
    zq = jnp.zeros((nq, nh, pad), F32)
    wq_main = jnp.concatenate([wq, zq], axis=-1).reshape(nq, nh * HEAD_SLAB)
    wq_rot = jnp.concatenate([jnp.zeros((nq, nh, MLA_NOPE), F32), _rot_half(wq[..., MLA_NOPE:]), zq],
                             axis=-1).reshape(nq, nh * HEAD_SLAB)
    wkv = w_ukv.reshape(nkv, nh, MLA_NOPE + MLA_V)
    wk = jnp.concatenate([wkv[..., :MLA_NOPE], jnp.zeros((nkv, nh, HEAD_SLAB - MLA_NOPE), F32)],
                         axis=-1).reshape(nkv, nh * HEAD_SLAB)
    wv = wkv[..., MLA_NOPE:].reshape(nkv, nh * MLA_V)

    pos = jnp.arange(seq, dtype=F32)
    inv_freq = ROPE_THETA ** (-jnp.arange(0, MLA_ROPE, 2, dtype=F32) / MLA_ROPE)
    ang = pos[:, None] * inv_freq[None, :]
    cos, sin = jnp.cos(ang), jnp.sin(ang)
    cos_t = jnp.concatenate([jnp.ones((seq, MLA_NOPE), F32), cos, cos, jnp.zeros((seq, pad), F32)], axis=1)
    sin_t = jnp.concatenate([jnp.zeros((seq, MLA_NOPE), F32), sin, sin, jnp.zeros((seq, pad), F32)], axis=1)

    tm = _tile(seq, 512)
    spt = seq // tm
    row = lambda i: (i, 0)
    fixed = lambda i: (0, 0)
    nlat = win_ext.shape[1]
    q, k, v = pl.pallas_call(
        functools.partial(_mla_proj_kernel, float((MLA_NOPE + MLA_ROPE) ** -0.5)),
        grid=(t // tm,),
        in_specs=[pl.BlockSpec((tm, d), row), pl.BlockSpec((1, d), fixed),
                  pl.BlockSpec((d, nlat), fixed),
                  pl.BlockSpec((1, nq), fixed), pl.BlockSpec((1, nkv), fixed),
                  pl.BlockSpec((nq, nh * HEAD_SLAB), fixed), pl.BlockSpec((nq, nh * HEAD_SLAB), fixed),
                  pl.BlockSpec((nkv, nh * HEAD_SLAB), fixed), pl.BlockSpec((nkv, nh * MLA_V), fixed),
                  pl.BlockSpec((tm, HEAD_SLAB), lambda i: (i % spt, 0)),
                  pl.BlockSpec((tm, HEAD_SLAB), lambda i: (i % spt, 0))],
        out_specs=[pl.BlockSpec((tm, nh * HEAD_SLAB), row), pl.BlockSpec((tm, nh * HEAD_SLAB), row),
                   pl.BlockSpec((tm, nh * MLA_V), row)],
        out_shape=[jax.ShapeDtypeStruct((t, nh * HEAD_SLAB), BF16),
                   jax.ShapeDtypeStruct((t, nh * HEAD_SLAB), BF16),
                   jax.ShapeDtypeStruct((t, nh * MLA_V), BF16)],
        compiler_params=_params("parallel"),
        name="mla_proj",
    )(x2d, g.reshape(1, d), win_ext.astype(BF16), q_norm.reshape(1, nq), kv_norm.reshape(1, nkv),
      wq_main.astype(BF16), wq_rot.astype(BF16), wk.astype(BF16), wv.astype(BF16), cos_t, sin_t)

    tq = _tile(seq, 512)
    qpt = seq // tq
    o = pl.pallas_call(
        _attn_kernel,
        grid=(bsz, nh // 2, qpt),
        in_specs=[pl.BlockSpec((tq, 2 * HEAD_SLAB), lambda b, hp, qi: (b * qpt + qi, hp)),
                  pl.BlockSpec((seq, 2 * HEAD_SLAB), lambda b, hp, qi: (b, hp)),
                  pl.BlockSpec((seq, 2 * MLA_V), lambda b, hp, qi: (b, hp))],
        out_specs=pl.BlockSpec((tq, 2 * MLA_V), lambda b, hp, qi: (b * qpt + qi, hp)),
        out_shape=jax.ShapeDtypeStruct((t, nh * MLA_V), BF16),
        compiler_params=_params("parallel", "parallel", "arbitrary"),
        name="mla_attention",
    )(q, k, v)

    tmo = _tile(t, 1024)
    return pl.pallas_call(
        _out_proj_kernel,
        grid=(t // tmo,),
        in_specs=[pl.BlockSpec((tmo, nh * MLA_V), row), pl.BlockSpec((nh * MLA_V, d), fixed),
                  pl.BlockSpec((tmo, d), row)],
        out_specs=pl.BlockSpec((tmo, d), row),
        out_shape=jax.ShapeDtypeStruct((t, d), F32),
        compiler_params=_params("parallel"),
        name="mla_out_proj",
    )(o, w_o.astype(BF16), x2d)


def kernel(x, norm_mix, norm_ffn, conv_w_in, conv_w, conv_w_out, mla_w_in, mla_q_norm, mla_kv_norm,
           mla_w_uq, mla_w_ukv, mla_w_o, peer_w_q, peer_k1, peer_k2, peer_u, peer_v, final_norm):
    bsz, seq, d = x.shape
    depth = norm_mix.shape[0]
    h = x.reshape(bsz * seq, d)
    for i in range(depth):
        j = i // 2
        if i % 2 == 0:
            h = _conv_mixer(h, seq, norm_mix[i], conv_w_in[j], conv_w[j], conv_w_out[j])
        else:
            h = _mla_mixer(h, bsz, seq, norm_mix[i], mla_w_in[j], mla_q_norm[j], mla_kv_norm[j],
                           mla_w_uq[j], mla_w_ukv[j], mla_w_o[j])
        h = _peer_ffn(h, norm_ffn[i], peer_w_q[i], peer_k1[i], peer_k2[i], peer_u[i], peer_v[i],
                      final_norm if i == depth - 1 else None)
    return h.reshape(bsz, seq, d)
```

```python
import functools

import jax
import jax.numpy as jnp
from jax import lax
from jax.experimental import pallas as pl
from jax.experimental.pallas import tpu as pltpu

F32 = jnp.float32
BF16 = jnp.bfloat16

NORM_EPS = 1e-6
CONV_WIDTH = 3
MLA_HEADS = 16
MLA_Q_LORA = 384
MLA_KV_LORA = 256
MLA_NOPE = 64
MLA_ROPE = 32
MLA_V = 64
ROPE_THETA = 10000.0
PEER_HEADS = 8
PEER_N_KEYS = 128
PEER_TOPK = 16
PEER_D_HALF = 128

LANES = 128
SUBLANES = 8
HEAD_SLAB = 128
VMEM_LIMIT = 56 * 1024 * 1024
EXPERT_CHUNKS = 2
ROUTE_GROUP = 4
GATE_UNROLL = 16

INV_SQRT2 = 0.7071067811865476
ID_SENTINEL = 1e9
_NT = (((1,), (1,)), ((), ()))


def _tile(n, pref):
    t = min(n, pref)
    while n % t:
        t -= 1
    return t


def _params(*sem):
    return pltpu.CompilerParams(dimension_semantics=sem, vmem_limit_bytes=VMEM_LIMIT)


def _rms(x, g):
    return x * lax.rsqrt(jnp.mean(x * x, axis=-1, keepdims=True) + NORM_EPS) * g


def _conv_in_kernel(x_ref, g_ref, w_ref, u_ref, b_ref):
    d = x_ref.shape[1]
    xn = _rms(x_ref[...], g_ref[...]).astype(BF16)
    b_ref[...] = jnp.dot(xn, w_ref[:, :d], preferred_element_type=F32)
    c = jnp.dot(xn, w_ref[:, d:2 * d], preferred_element_type=F32)
    h = jnp.dot(xn, w_ref[:, 2 * d:], preferred_element_type=F32)
    u_ref[...] = c * h


def _conv_out_kernel(seq_tiles, u_ref, up_ref, un_ref, b_ref, x_ref, cw_ref, w_ref, o_ref):
    i = pl.program_id(0)
    tm = u_ref.shape[0]
    u = u_ref[...]
    first = (i % seq_tiles) == 0
    last = (i % seq_tiles) == seq_tiles - 1
    prev_row = jnp.where(first, 0.0, up_ref[SUBLANES - 1:SUBLANES, :])
    next_row = jnp.where(last, 0.0, un_ref[0:1, :])
    row = lax.broadcasted_iota(jnp.int32, u.shape, 0)
    u_m = jnp.where(row == 0, prev_row, pltpu.roll(u, 1, 0))
    u_p = jnp.where(row == tm - 1, next_row, pltpu.roll(u, tm - 1, 0))
    y = cw_ref[0:1, :] * u_m + cw_ref[1:2, :] * u + cw_ref[2:3, :] * u_p
    z = (b_ref[...] * y).astype(BF16)
    o_ref[...] = x_ref[...] + jnp.dot(z, w_ref[...], preferred_element_type=F32)


def _conv_mixer(x2d, seq, g, w_in, conv_w, w_out):
    t, d = x2d.shape
    tm = _tile(seq, 512)
    nt = t // tm
    row = lambda i: (i, 0)
    fixed = lambda i: (0, 0)
    u, b = pl.pallas_call(
        _conv_in_kernel,
        grid=(nt,),
        in_specs=[pl.BlockSpec((tm, d), row), pl.BlockSpec((1, d), fixed),
                  pl.BlockSpec((d, 3 * d), fixed)],
        out_specs=[pl.BlockSpec((tm, d), row), pl.BlockSpec((tm, d), row)],
        out_shape=[jax.ShapeDtypeStruct((t, d), F32)] * 2,
        compiler_params=_params("parallel"),
        name="conv_in",
    )(x2d, g.reshape(1, d), w_in.astype(BF16))
    r8 = tm // SUBLANES
    n8 = t // SUBLANES
    return pl.pallas_call(
        functools.partial(_conv_out_kernel, seq // tm),
        grid=(nt,),
        in_specs=[pl.BlockSpec((tm, d), row),
                  pl.BlockSpec((SUBLANES, d), lambda i: (jnp.maximum(i * r8 - 1, 0), 0)),
                  pl.BlockSpec((SUBLANES, d), lambda i: (jnp.minimum((i + 1) * r8, n8 - 1), 0)),
                  pl.BlockSpec((tm, d), row), pl.BlockSpec((tm, d), row),
                  pl.BlockSpec((CONV_WIDTH, d), fixed), pl.BlockSpec((d, d), fixed)],
        out_specs=pl.BlockSpec((tm, d), row),
        out_shape=jax.ShapeDtypeStruct((t, d), F32),
        compiler_params=_params("parallel"),
        name="conv_out",
    )(u, u, u, b, x2d, conv_w, w_out.astype(BF16))


def _topk_rows(s, k):
    n = s.shape[0]
    iota = lax.broadcasted_iota(jnp.int32, s.shape, 0).astype(F32)
    vals, idxs = [], []
    for _ in range(k):
        m = jnp.max(s, axis=0, keepdims=True)
        idx = jnp.min(jnp.where(s == m, iota, float(n)), axis=0, keepdims=True)
        vals.append(m)
        idxs.append(idx)
        s = jnp.where(iota == idx, -jnp.inf, s)
    return vals, idxs


def _pair_topk(v1, i1, v2, i2, k, n_keys):
    g = SUBLANES
    w = lax.broadcasted_iota(jnp.int32, (g, v1[0].shape[1]), 0)
    wf = w.astype(F32)
    cat = lambda rows: jnp.concatenate(rows, axis=0)
    neg = -jnp.inf
    cands, poss, eids = [], [], []
    for r in range(k):
        cmax = k // (r + 1)
        if cmax < 4:
            break
        for c0 in range(0, cmax, g):
            val = v1[r] + cat(v2[c0:c0 + g])
            eid = i1[r] * float(n_keys) + cat(i2[c0:c0 + g])
            if c0 + g > cmax:
                val = jnp.where(w < cmax - c0, val, neg)
            cands.append(val)
            poss.append(wf + float(r * k + c0))
            eids.append(eid)
    r_done = r
    for c in range(k):
        rmax = k // (c + 1)
        if rmax <= r_done:
            break
        for r0 in range((r_done // g) * g, rmax, g):
            val = cat(v1[r0:r0 + g]) + v2[c]
            eid = cat(i1[r0:r0 + g]) * float(n_keys) + i2[c]
            lo, hi = max(r_done - r0, 0), min(rmax - r0, g)
            if lo > 0 or hi < g:
                val = jnp.where((w >= lo) & (w < hi), val, neg)
            cands.append(val)
            poss.append(wf * float(k) + float(r0 * k + c))
            eids.append(eid)
    cand, pos, eid = cat(cands), cat(poss), cat(eids)
    tops, sel_e = [], []
    for _ in range(k):
        m = jnp.max(cand, axis=0, keepdims=True)
        p = jnp.min(jnp.where(cand == m, pos, float(k * k)), axis=0, keepdims=True)
        hit = pos == p
        tops.append(m)
        sel_e.append(jnp.max(jnp.where(hit, eid, -1.0), axis=0, keepdims=True))
        cand = jnp.where(hit, neg, cand)
    return tops, sel_e


def _batcher_pairs(n):
    pairs = []
    p = 1
    while p < n:
        k = p
        while k >= 1:
            for j in range(k % p, n - k, 2 * k):
                for i in range(min(k, n - j - k)):
                    if (i + j) // (2 * p) == (i + j + k) // (2 * p):
                        pairs.append((i + j, i + j + k))
            k //= 2
        p *= 2
    return pairs


def _all_sublanes(op, x):
    for shift in (4, 2, 1):
        x = op(x, pltpu.roll(x, shift, 0))
    return x


def _sort_classes(s):
    g = SUBLANES
    n = s.shape[0] // g
    sub = lax.broadcasted_iota(jnp.int32, (g, s.shape[1]), 0).astype(F32)
    vals = [s[v * g:(v + 1) * g, :] for v in range(n)]
    ids = [sub + float(v * g) for v in range(n)]
    for a, b in _batcher_pairs(n):
        keep = vals[a] >= vals[b]
        vals[a], vals[b] = jnp.maximum(vals[a], vals[b]), jnp.minimum(vals[a], vals[b])
        ids[a], ids[b] = jnp.where(keep, ids[a], ids[b]), jnp.where(keep, ids[b], ids[a])
    return vals, ids


def _merge_lists(vals, ids, k):
    neg = jnp.full_like(vals[0], -jnp.inf)
    vals = list(vals) + [neg]
    ids = list(ids) + [ids[0]]
    out_v, out_i = [], []
    tie = jnp.zeros_like(vals[0])
    for it in range(k + 1):
        m = _all_sublanes(jnp.maximum, vals[0])
        idw = _all_sublanes(jnp.minimum, jnp.where(vals[0] == m, ids[0], ID_SENTINEL))
        if out_v:
            tie = jnp.where(m == out_v[-1], 1.0, tie)
        out_v.append(m)
        out_i.append(idw)
        win = ids[0] == idw
        for r in range(min(k - it, len(vals) - 1)):
            vals[r] = jnp.where(win, vals[r + 1], vals[r])
            ids[r] = jnp.where(win, ids[r + 1], ids[r])
    return out_v, out_i, tie


def _rows_to_tile(rows):
    sub = lax.broadcasted_iota(jnp.int32, rows[0].shape, 0)
    out = rows[-1]
    for j in range(len(rows) - 2, -1, -1):
        out = jnp.where(sub == j, rows[j], out)
    return out


def _pair_merge(v1, i1, v2, i2, k, n_keys):
    g = SUBLANES
    n_row = 4
    assert k == 16 and g == 8 and k // (n_row + 1) < n_row <= k // n_row
    n_col = k // (n_row + 1)
    lens = [k // (r + 1) for r in range(n_row)] + [k // (c + 1) - n_row for c in range(n_col)]
    sub = lax.broadcasted_iota(jnp.int32, v1[0].shape, 0)
    is_row = sub < n_row
    length = jnp.zeros(v1[0].shape, F32)
    for l, ln in enumerate(lens):
        length = jnp.where(sub == l, float(ln), length)
    pick = lambda rows, lo: _rows_to_tile([rows[max(j - lo, 0)] if j - lo < len(rows) else rows[-1]
                                            for j in range(g)])
    v1_row, i1_row = pick(v1[:n_row], 0), pick(i1[:n_row], 0)
    v2_col, i2_col = pick(v2[:n_col], n_row), pick(i2[:n_col], n_row)
    vals, eids = [], []
    for q in range(k):
        rq = min(n_row + q, k - 1)
        val = jnp.where(is_row, v1_row, v1[rq]) + jnp.where(is_row, v2[q], v2_col)
        eid = jnp.where(is_row, i1_row, i1[rq]) * float(n_keys) + jnp.where(is_row, i2[q], i2_col)
        vals.append(jnp.where(length > float(q), val, -jnp.inf))
        eids.append(eid)
    return _merge_lists(vals, eids, k)


def _route_kernel(x_ref, g_ref, wq_ref, k1_ref, k2_ref, xn_ref, e_ref, gate_ref, xn_s, q_s):
    h = pl.program_id(1)
    tm = x_ref.shape[0]
    nsub = tm // LANES

    @pl.when(h == 0)
    def _():
        xn = _rms(x_ref[...], g_ref[...]).astype(BF16)
        xn_s[...] = xn
        xn_ref[...] = xn

    q = jnp.dot(xn_s[...], wq_ref[...], preferred_element_type=F32)
    q_s[...] = q.astype(BF16).reshape(nsub, LANES, 2 * PEER_D_HALF)
    k1 = k1_ref[0]
    k2 = k2_ref[0]

    kk, g = PEER_TOPK, SUBLANES

    def scores(c):
        qc = q_s[c]
        s1 = lax.dot_general(k1, qc[:, :PEER_D_HALF], _NT, preferred_element_type=F32)
        s2 = lax.dot_general(k2, qc[:, PEER_D_HALF:], _NT, preferred_element_type=F32)
        return s1, s2

    def fast(c):
        s1, s2 = scores(c)
        v1, i1, tie1 = _merge_lists(*_sort_classes(s1), kk)
        v2, i2, tie2 = _merge_lists(*_sort_classes(s2), kk)
        tops, eids, tie3 = _pair_merge(v1, i1, v2, i2, kk, PEER_N_KEYS)
        top = jnp.concatenate([_rows_to_tile(tops[r:r + g]) for r in range(0, kk, g)], axis=0)
        eid = jnp.concatenate([_rows_to_tile(eids[r:r + g]) for r in range(0, kk, g)], axis=0)
        ex = jnp.exp(top - tops[0][:1])
        gate_ref[0, c] = ex / jnp.sum(ex, axis=0, keepdims=True)
        e_ref[0, c] = eid.astype(jnp.int32)
        return jnp.max(jnp.maximum(jnp.maximum(tie1, tie2), tie3))

    def exact(c):
        s1, s2 = scores(c)
        v1, i1 = _topk_rows(s1, kk)
        v2, i2 = _topk_rows(s2, kk)
        tops, eids = _pair_topk(v1, i1, v2, i2, kk, PEER_N_KEYS)
        top = jnp.concatenate(tops, axis=0)
        ex = jnp.exp(top - tops[0])
        gate_ref[0, c] = ex / jnp.sum(ex, axis=0, keepdims=True)
        e_ref[0, c] = jnp.concatenate(eids, axis=0).astype(jnp.int32)

    def sub_blocks(cg, carry):
        cs = [cg * ROUTE_GROUP + u for u in range(ROUTE_GROUP)]
        ties = [fast(c) for c in cs]
        for c, tie in zip(cs, ties):
            pl.when(tie > 0.0)(functools.partial(exact, c))
        return carry

    lax.fori_loop(0, nsub // ROUTE_GROUP, sub_blocks, 0)


def _peer_route(x2d, g, w_q, k1, k2):
    t, d = x2d.shape
    tm = _tile(t, 512)
    nsub = tm // LANES
    hq = 2 * PEER_D_HALF
    xn, eid, gate = pl.pallas_call(
        _route_kernel,
        grid=(t // tm, PEER_HEADS),
        in_specs=[pl.BlockSpec((tm, d), lambda i, h: (i, 0)),
                  pl.BlockSpec((1, d), lambda i, h: (0, 0)),
                  pl.BlockSpec((d, hq), lambda i, h: (0, h)),
                  pl.BlockSpec((1, PEER_N_KEYS, PEER_D_HALF), lambda i, h: (h, 0, 0)),
                  pl.BlockSpec((1, PEER_N_KEYS, PEER_D_HALF), lambda i, h: (h, 0, 0))],
        out_specs=[pl.BlockSpec((tm, d), lambda i, h: (i, 0)),
                   pl.BlockSpec((1, nsub, PEER_TOPK, LANES), lambda i, h: (h, i, 0, 0)),
                   pl.BlockSpec((1, nsub, PEER_TOPK, LANES), lambda i, h: (h, i, 0, 0))],
        out_shape=[jax.ShapeDtypeStruct((t, d), BF16),
                   jax.ShapeDtypeStruct((PEER_HEADS, t // LANES, PEER_TOPK, LANES), jnp.int32),
                   jax.ShapeDtypeStruct((PEER_HEADS, t // LANES, PEER_TOPK, LANES), F32)],
        scratch_shapes=[pltpu.VMEM((tm, d), BF16), pltpu.VMEM((nsub, LANES, hq), BF16)],
        compiler_params=_params("parallel", "arbitrary"),
        name="peer_route",
    )(x2d, g.reshape(1, d), w_q.astype(BF16), k1.astype(BF16), k2.astype(BF16))
    return xn, eid, gate


def _pack_bf16_pair(a, b):
    abits = lax.bitcast_convert_type(a.astype(BF16).astype(F32), jnp.uint32)
    bbits = lax.bitcast_convert_type(b.astype(BF16).astype(F32), jnp.uint32)
    return (abits >> 16) | (bbits & jnp.uint32(0xFFFF0000))


def _unpack_bf16_pair(w):
    lo = lax.bitcast_convert_type(w << 16, F32)
    hi = lax.bitcast_convert_type(w & jnp.uint32(0xFFFF0000), F32)
    return lo, hi


def _gate_matrix_kernel(e_ref, g_ref, o_ref):
    pb = e_ref.shape[0]
    nk = PEER_N_KEYS
    lane = lax.broadcasted_iota(jnp.int32, (1, 2 * nk), 1)
    second = jnp.where(lane >= nk, nk, 0)
    sub1 = lax.broadcasted_iota(jnp.int32, (nk, 2 * nk), 0)
    sub2 = lax.broadcasted_iota(jnp.int32, (2 * nk, 2 * nk), 0)

    def pair(p, carry):
        e = e_ref[pl.ds(p, 1), :]
        g = g_ref[pl.ds(p, 1), :]
        i1 = e >> 7
        i2 = (e & (nk - 1)) + second
        lhs = jnp.where(sub1 == i1, g, 0.0).astype(BF16)
        rhs = jnp.where(sub2 == i2, 1.0, 0.0).astype(BF16)
        res = lax.dot_general(lhs, rhs, _NT, preferred_element_type=F32)
        packed = _pack_bf16_pair(res[:, :nk], res[:, nk:])
        for c in range(nk // SUBLANES):
            o_ref[c, p] = packed[c * SUBLANES:(c + 1) * SUBLANES, :]
        return carry

    lax.fori_loop(0, pb, pair, 0, unroll=GATE_UNROLL)


def _gate_matrix(eid, gate, tm):
    hh, nb, kk, ln = eid.shape
    t = nb * ln
    nsel = hh * kk
    half = tm // 2
    nchunk = PEER_N_KEYS // SUBLANES
    pb = _tile(half, 64)

    def pairs(a):
        a = a.transpose(1, 3, 0, 2).reshape(t // tm, 2, half, nsel)
        return a.transpose(0, 2, 1, 3).reshape(t // 2, 2 * nsel)

    spt = half // pb
    wp = pl.pallas_call(
        _gate_matrix_kernel,
        grid=(t // tm, spt),
        in_specs=[pl.BlockSpec((pb, 2 * nsel), lambda i, s: (i * spt + s, 0)),
                  pl.BlockSpec((pb, 2 * nsel), lambda i, s: (i * spt + s, 0))],
        out_specs=pl.BlockSpec((nchunk, pb, SUBLANES, PEER_N_KEYS), lambda i, s: (i, s, 0, 0)),
        out_shape=jax.ShapeDtypeStruct((t // tm * nchunk, half, SUBLANES, PEER_N_KEYS), jnp.uint32),
        compiler_params=_params("parallel", "parallel"),
        name="peer_gate_matrix",
    )(pairs(eid), pairs(gate))
    return wp.reshape(t // tm * nchunk * half * SUBLANES, PEER_N_KEYS)


def _experts_kernel(final, xn_ref, ut_ref, v_ref, w_ref, x_ref, fg_ref, o_ref, acc_ref):
    j = pl.program_id(1)
    half = x_ref.shape[0] // 2

    @pl.when(j == 0)
    def _():
        acc_ref[...] = x_ref[...]

    a = jnp.dot(xn_ref[...], ut_ref[...], preferred_element_type=F32)
    cols = []
    for chunk in range(EXPERT_CHUNKS):
        for c in range(SUBLANES):
            wc = w_ref[pl.ds(chunk * half * SUBLANES + c, half, stride=SUBLANES), :]
            cols.append(jnp.concatenate(_unpack_bf16_pair(wc), axis=0))
    w = jnp.concatenate(cols, axis=1)
    gelu = 0.5 * a * (1.0 + lax.erf(a * INV_SQRT2))
    hid = (gelu * w).astype(BF16)
    acc_ref[...] += jnp.dot(hid, v_ref[...], preferred_element_type=F32)

    @pl.when(j == pl.num_programs(1) - 1)
    def _():
        y = acc_ref[...]
        o_ref[...] = _rms(y, fg_ref[...]) if final else y


def _peer_experts(x2d, xn, w, u_t, v, final_gain, tm):
    t, d = x2d.shape
    ne = v.shape[0]
    te = EXPERT_CHUNKS * SUBLANES * PEER_N_KEYS
    nj = ne // te
    final = final_gain is not None
    fg = (final_gain if final else jnp.ones((d,), F32)).reshape(1, d)
    return pl.pallas_call(
        functools.partial(_experts_kernel, final),
        grid=(t // tm, nj),
        in_specs=[pl.BlockSpec((tm, d), lambda i, j: (i, 0)),
                  pl.BlockSpec((d, te), lambda i, j: (0, j)),
                  pl.BlockSpec((te, d), lambda i, j: (j, 0)),
                  pl.BlockSpec((EXPERT_CHUNKS * tm // 2 * SUBLANES, PEER_N_KEYS), lambda i, j: (i * nj + j, 0)),
                  pl.BlockSpec((tm, d), lambda i, j: (i, 0)),
                  pl.BlockSpec((1, d), lambda i, j: (0, 0))],
        out_specs=pl.BlockSpec((tm, d), lambda i, j: (i, 0)),
        out_shape=jax.ShapeDtypeStruct((t, d), F32),
        scratch_shapes=[pltpu.VMEM((tm, d), F32)],
        compiler_params=_params("parallel", "arbitrary"),
        name="peer_experts",
    )(xn, u_t, v, w, x2d, fg)


def _peer_ffn(x2d, g, w_q, k1, k2, u_tab, v_tab, final_gain=None):
    tm = _tile(x2d.shape[0], 512)
    xn, eid, gate = _peer_route(x2d, g, w_q, k1, k2)
    w = _gate_matrix(eid, gate, tm)
    return _peer_experts(x2d, xn, w, u_tab.astype(BF16).T, v_tab.astype(BF16), final_gain, tm)


def _mla_proj_kernel(scale, x_ref, g_ref, win_ref, qg_ref, kvg_ref, wqm_ref, wqr_ref, wk_ref, wv_ref,
                     cos_ref, sin_ref, q_ref, k_ref, v_ref):
    xn = _rms(x_ref[...], g_ref[...]).astype(BF16)
    lat = jnp.dot(xn, win_ref[...], preferred_element_type=F32)
    nq, nkv = MLA_Q_LORA, MLA_KV_LORA
    c_q = _rms(lat[:, :nq], qg_ref[...]).astype(BF16)
    c_kv = _rms(lat[:, nq:nq + nkv], kvg_ref[...]).astype(BF16)
    cos = cos_ref[...]
    sin = sin_ref[...]
    k_rope = lat[:, nq + nkv:nq + nkv + HEAD_SLAB] * cos + lat[:, nq + nkv + HEAD_SLAB:] * sin
    qm = jnp.dot(c_q, wqm_ref[...], preferred_element_type=F32)
    qr = jnp.dot(c_q, wqr_ref[...], preferred_element_type=F32)
    kn = jnp.dot(c_kv, wk_ref[...], preferred_element_type=F32)
    v_ref[...] = jnp.dot(c_kv, wv_ref[...], preferred_element_type=F32).astype(v_ref.dtype)
    for h in range(MLA_HEADS):
        sl = slice(h * HEAD_SLAB, (h + 1) * HEAD_SLAB)
        q_ref[:, sl] = ((qm[:, sl] * cos + qr[:, sl] * sin) * scale).astype(q_ref.dtype)
        k_ref[:, sl] = (kn[:, sl] + k_rope).astype(k_ref.dtype)


def _attn_kernel(q_ref, k_ref, v_ref, o_ref):
    v2 = v_ref[...]
    outs = []
    for hh in range(2):
        sl = slice(hh * HEAD_SLAB, (hh + 1) * HEAD_SLAB)
        s = lax.dot_general(q_ref[:, sl], k_ref[:, sl], _NT, preferred_element_type=F32)
        m = jnp.max(s, axis=-1, keepdims=True)
        p = jnp.exp(s - m)
        l = jnp.sum(p, axis=-1, keepdims=True)
        pv = jnp.dot(p.astype(BF16), v2, preferred_element_type=F32)
        outs.append(pv / l)
    lane = lax.broadcasted_iota(jnp.int32, outs[0].shape, 1)
    o_ref[...] = jnp.where(lane < MLA_V, outs[0], outs[1]).astype(o_ref.dtype)


def _out_proj_kernel(o_ref, w_ref, x_ref, y_ref):
    y_ref[...] = x_ref[...] + jnp.dot(o_ref[...], w_ref[...], preferred_element_type=F32)


def _rot_half(w):
    hr = MLA_ROPE // 2
    return jnp.concatenate([-w[..., hr:], w[..., :hr]], axis=-1)


def _mla_mixer(x2d, bsz, seq, g, w_in, q_norm, kv_norm, w_uq, w_ukv, w_o):
    t, d = x2d.shape
    nh, nq, nkv = MLA_HEADS, MLA_Q_LORA, MLA_KV_LORA
    pad = HEAD_SLAB - MLA_NOPE - MLA_ROPE

    w_kr = w_in[:, nq + nkv:]
    zl = jnp.zeros((d, MLA_NOPE), F32)
    zr = jnp.zeros((d, pad), F32)
    win_ext = jnp.concatenate([w_in[:, :nq + nkv], zl, w_kr, zr, zl, _rot_half(w_kr), zr], axis=1)
    wq = w_uq.reshape(nq, nh, MLA_NOPE + MLA_ROPE)
    zq = jnp.zeros((nq, nh, pad), F32)
    wq_main = jnp.concatenate([wq, zq], axis=-1).reshape(nq, nh * HEAD_SLAB)
    wq_rot = jnp.concatenate([jnp.zeros((nq, nh, MLA_NOPE), F32), _rot_half(wq[..., MLA_NOPE:]), zq],
                             axis=-1).reshape(nq, nh * HEAD_SLAB)
    wkv = w_ukv.reshape(nkv, nh, MLA_NOPE + MLA_V)
    wk = jnp.concatenate([wkv[..., :MLA_NOPE], jnp.zeros((nkv, nh, HEAD_SLAB - MLA_NOPE), F32)],
                         axis=-1).reshape(nkv, nh * HEAD_SLAB)
    wv = wkv[..., MLA_NOPE:].reshape(nkv, nh * MLA_V)

    pos = jnp.arange(seq, dtype=F32)
    inv_freq = ROPE_THETA ** (-jnp.arange(0, MLA_ROPE, 2, dtype=F32) / MLA_ROPE)
    ang = pos[:, None] * inv_freq[None, :]
    cos, sin = jnp.cos(ang), jnp.sin(ang)
    cos_t = jnp.concatenate([jnp.ones((seq, MLA_NOPE), F32), cos, cos, jnp.zeros((seq, pad), F32)], axis=1)
    sin_t = jnp.concatenate([jnp.zeros((seq, MLA_NOPE), F32), sin, sin, jnp.zeros((seq, pad), F32)], axis=1)

    tm = _tile(seq, 512)
    spt = seq // tm
    row = lambda i: (i, 0)
    fixed = lambda i: (0, 0)
    nlat = win_ext.shape[1]
    q, k, v = pl.pallas_call(
        functools.partial(_mla_proj_kernel, float((MLA_NOPE + MLA_ROPE) ** -0.5)),
        grid=(t // tm,),
        in_specs=[pl.BlockSpec((tm, d), row), pl.BlockSpec((1, d), fixed),
                  pl.BlockSpec((d, nlat), fixed),
                  pl.BlockSpec((1, nq), fixed), pl.BlockSpec((1, nkv), fixed),
                  pl.BlockSpec((nq, nh * HEAD_SLAB), fixed), pl.BlockSpec((nq, nh * HEAD_SLAB), fixed),
                  pl.BlockSpec((nkv, nh * HEAD_SLAB), fixed), pl.BlockSpec((nkv, nh * MLA_V), fixed),
                  pl.BlockSpec((tm, HEAD_SLAB), lambda i: (i % spt, 0)),
                  pl.BlockSpec((tm, HEAD_SLAB), lambda i: (i % spt, 0))],
        out_specs=[pl.BlockSpec((tm, nh * HEAD_SLAB), row), pl.BlockSpec((tm, nh * HEAD_SLAB), row),
                   pl.BlockSpec((tm, nh * MLA_V), row)],
        out_shape=[jax.ShapeDtypeStruct((t, nh * HEAD_SLAB), BF16),
                   jax.ShapeDtypeStruct((t, nh * HEAD_SLAB), BF16),
                   jax.ShapeDtypeStruct((t, nh * MLA_V), BF16)],
        compiler_params=_params("parallel"),
        name="mla_proj",
    )(x2d, g.reshape(1, d), win_ext.astype(BF16), q_norm.reshape(1, nq), kv_norm.reshape(1, nkv),
      wq_main.astype(BF16), wq_rot.astype(BF16), wk.astype(BF16), wv.astype(BF16), cos_t, sin_t)

    tq = _tile(seq, 512)
    qpt = seq // tq
    o = pl.pallas_call(
        _attn_kernel,
        grid=(bsz, nh // 2, qpt),
        in_specs=[pl.BlockSpec((tq, 2 * HEAD_SLAB), lambda b, hp, qi: (b * qpt + qi, hp)),
                  pl.BlockSpec((seq, 2 * HEAD_SLAB), lambda b, hp, qi: (b, hp)),
                  pl.BlockSpec((seq, 2 * MLA_V), lambda b, hp, qi: (b, hp))],
        out_specs=pl.BlockSpec((tq, 2 * MLA_V), lambda b, hp, qi: (b * qpt + qi, hp)),
        out_shape=jax.ShapeDtypeStruct((t, nh * MLA_V), BF16),
        compiler_params=_params("parallel", "parallel", "arbitrary"),
        name="mla_attention",
    )(q, k, v)

    tmo = _tile(t, 1024)
    return pl.pallas_call(
        _out_proj_kernel,
        grid=(t // tmo,),
        in_specs=[pl.BlockSpec((tmo, nh * MLA_V), row), pl.BlockSpec((nh * MLA_V, d), fixed),
                  pl.BlockSpec((tmo, d), row)],
        out_specs=pl.BlockSpec((tmo, d), row),
        out_shape=jax.ShapeDtypeStruct((t, d), F32),
        compiler_params=_params("parallel"),
        name="mla_out_proj",
    )(o, w_o.astype(BF16), x2d)


def kernel(x, norm_mix, norm_ffn, conv_w_in, conv_w, conv_w_out, mla_w_in, mla_q_norm, mla_kv_norm,
           mla_w_uq, mla_w_ukv, mla_w_o, peer_w_q, peer_k1, peer_k2, peer_u, peer_v, final_norm):
    bsz, seq, d = x.shape
    depth = norm_mix.shape[0]
    h = x.reshape(bsz * seq, d)
    for i in range(depth):
        j = i // 2
        if i % 2 == 0:
            h = _conv_mixer(h, seq, norm_mix[i], conv_w_in[j], conv_w[j], conv_w_out[j])
        else:
            h = _mla_mixer(h, bsz, seq, norm_mix[i], mla_w_in[j], mla_q_norm[j], mla_kv_norm[j],
                           mla_w_uq[j], mla_w_ukv[j], mla_w_o[j])
        h = _peer_ffn(h, norm_ffn[i], peer_w_q[i], peer_k1[i], peer_k2[i], peer_u[i], peer_v[i],
                      final_norm if i == depth - 1 else None)
    return h.reshape(bsz, seq, d)
```

```python
import functools

import jax
import jax.numpy as jnp
from jax import lax
from jax.experimental import pallas as pl
from jax.experimental.pallas import tpu as pltpu

F32 = jnp.float32
BF16 = jnp.bfloat16

NORM_EPS = 1e-6
CONV_WIDTH = 3
MLA_HEADS = 16
MLA_Q_LORA = 384
MLA_KV_LORA = 256
MLA_NOPE = 64
MLA_ROPE = 32
MLA_V = 64
ROPE_THETA = 10000.0
PEER_HEADS = 8
PEER_N_KEYS = 128
PEER_TOPK = 16
PEER_D_HALF = 128

LANES = 128
SUBLANES = 8
HEAD_SLAB = 128
VMEM_LIMIT = 56 * 1024 * 1024
EXPERT_CHUNKS = 2
ROUTE_GROUP = 4
GATE_UNROLL = 32

INV_SQRT2 = 0.7071067811865476
ID_SENTINEL = 1e9
_NT = (((1,), (1,)), ((), ()))


def _tile(n, pref):
    t = min(n, pref)
    while n % t:
        t -= 1
    return t


def _params(*sem):
    return pltpu.CompilerParams(dimension_semantics=sem, vmem_limit_bytes=VMEM_LIMIT)


def _rms(x, g):
    return x * lax.rsqrt(jnp.mean(x * x, axis=-1, keepdims=True) + NORM_EPS) * g


def _conv_in_kernel(x_ref, g_ref, w_ref, u_ref, b_ref):
    d = x_ref.shape[1]
    xn = _rms(x_ref[...], g_ref[...]).astype(BF16)
    b_ref[...] = jnp.dot(xn, w_ref[:, :d], preferred_element_type=F32)
    c = jnp.dot(xn, w_ref[:, d:2 * d], preferred_element_type=F32)
    h = jnp.dot(xn, w_ref[:, 2 * d:], preferred_element_type=F32)
    u_ref[...] = c * h


def _conv_out_kernel(seq_tiles, u_ref, up_ref, un_ref, b_ref, x_ref, cw_ref, w_ref, o_ref):
    i = pl.program_id(0)
    tm = u_ref.shape[0]
    u = u_ref[...]
    first = (i % seq_tiles) == 0
    last = (i % seq_tiles) == seq_tiles - 1
    prev_row = jnp.where(first, 0.0, up_ref[SUBLANES - 1:SUBLANES, :])
    next_row = jnp.where(last, 0.0, un_ref[0:1, :])
    row = lax.broadcasted_iota(jnp.int32, u.shape, 0)
    u_m = jnp.where(row == 0, prev_row, pltpu.roll(u, 1, 0))
    u_p = jnp.where(row == tm - 1, next_row, pltpu.roll(u, tm - 1, 0))
    y = cw_ref[0:1, :] * u_m + cw_ref[1:2, :] * u + cw_ref[2:3, :] * u_p
    z = (b_ref[...] * y).astype(BF16)
    o_ref[...] = x_ref[...] + jnp.dot(z, w_ref[...], preferred_element_type=F32)


def _conv_mixer(x2d, seq, g, w_in, conv_w, w_out):
    t, d = x2d.shape
    tm = _tile(seq, 512)
    nt = t // tm
    row = lambda i: (i, 0)
    fixed = lambda i: (0, 0)
    u, b = pl.pallas_call(
        _conv_in_kernel,
        grid=(nt,),
        in_specs=[pl.BlockSpec((tm, d), row), pl.BlockSpec((1, d), fixed),
                  pl.BlockSpec((d, 3 * d), fixed)],
        out_specs=[pl.BlockSpec((tm, d), row), pl.BlockSpec((tm, d), row)],
        out_shape=[jax.ShapeDtypeStruct((t, d), F32)] * 2,
        compiler_params=_params("parallel"),
        name="conv_in",
    )(x2d, g.reshape(1, d), w_in.astype(BF16))
    r8 = tm // SUBLANES
    n8 = t // SUBLANES
    return pl.pallas_call(
        functools.partial(_conv_out_kernel, seq // tm),
        grid=(nt,),
        in_specs=[pl.BlockSpec((tm, d), row),
                  pl.BlockSpec((SUBLANES, d), lambda i: (jnp.maximum(i * r8 - 1, 0), 0)),
                  pl.BlockSpec((SUBLANES, d), lambda i: (jnp.minimum((i + 1) * r8, n8 - 1), 0)),
                  pl.BlockSpec((tm, d), row), pl.BlockSpec((tm, d), row),
                  pl.BlockSpec((CONV_WIDTH, d), fixed), pl.BlockSpec((d, d), fixed)],
        out_specs=pl.BlockSpec((tm, d), row),
        out_shape=jax.ShapeDtypeStruct((t, d), F32),
        compiler_params=_params("parallel"),
        name="conv_out",
    )(u, u, u, b, x2d, conv_w, w_out.astype(BF16))


def _topk_rows(s, k):
    n = s.shape[0]
    iota = lax.broadcasted_iota(jnp.int32, s.shape, 0).astype(F32)
    vals, idxs = [], []
    for _ in range(k):
        m = jnp.max(s, axis=0, keepdims=True)
        idx = jnp.min(jnp.where(s == m, iota, float(n)), axis=0, keepdims=True)
        vals.append(m)
        idxs.append(idx)
        s = jnp.where(iota == idx, -jnp.inf, s)
    return vals, idxs


def _pair_topk(v1, i1, v2, i2, k, n_keys):
    g = SUBLANES
    w = lax.broadcasted_iota(jnp.int32, (g, v1[0].shape[1]), 0)
    wf = w.astype(F32)
    cat = lambda rows: jnp.concatenate(rows, axis=0)
    neg = -jnp.inf
    cands, poss, eids = [], [], []
    for r in range(k):
        cmax = k // (r + 1)
        if cmax < 4:
            break
        for c0 in range(0, cmax, g):
            val = v1[r] + cat(v2[c0:c0 + g])
            eid = i1[r] * float(n_keys) + cat(i2[c0:c0 + g])
            if c0 + g > cmax:
                val = jnp.where(w < cmax - c0, val, neg)
            cands.append(val)
            poss.append(wf + float(r * k + c0))
            eids.append(eid)
    r_done = r
    for c in range(k):
        rmax = k // (c + 1)
        if rmax <= r_done:
            break
        for r0 in range((r_done // g) * g, rmax, g):
            val = cat(v1[r0:r0 + g]) + v2[c]
            eid = cat(i1[r0:r0 + g]) * float(n_keys) + i2[c]
            lo, hi = max(r_done - r0, 0), min(rmax - r0, g)
            if lo > 0 or hi < g:
                val = jnp.where((w >= lo) & (w < hi), val, neg)
            cands.append(val)
            poss.append(wf * float(k) + float(r0 * k + c))
            eids.append(eid)
    cand, pos, eid = cat(cands), cat(poss), cat(eids)
    tops, sel_e = [], []
    for _ in range(k):
        m = jnp.max(cand, axis=0, keepdims=True)
        p = jnp.min(jnp.where(cand == m, pos, float(k * k)), axis=0, keepdims=True)
        hit = pos == p
        tops.append(m)
        sel_e.append(jnp.max(jnp.where(hit, eid, -1.0), axis=0, keepdims=True))
        cand = jnp.where(hit, neg, cand)
    return tops, sel_e


def _batcher_pairs(n):
    pairs = []
    p = 1
    while p < n:
        k = p
        while k >= 1:
            for j in range(k % p, n - k, 2 * k):
                for i in range(min(k, n - j - k)):
                    if (i + j) // (2 * p) == (i + j + k) // (2 * p):
                        pairs.append((i + j, i + j + k))
            k //= 2
        p *= 2
    return pairs


def _all_sublanes(op, x):
    for shift in (4, 2, 1):
        x = op(x, pltpu.roll(x, shift, 0))
    return x


def _sort_classes(s):
    g = SUBLANES
    n = s.shape[0] // g
    sub = lax.broadcasted_iota(jnp.int32, (g, s.shape[1]), 0).astype(F32)
    vals = [s[v * g:(v + 1) * g, :] for v in range(n)]
    ids = [sub + float(v * g) for v in range(n)]
    for a, b in _batcher_pairs(n):
        keep = vals[a] >= vals[b]
        vals[a], vals[b] = jnp.maximum(vals[a], vals[b]), jnp.minimum(vals[a], vals[b])
        ids[a], ids[b] = jnp.where(keep, ids[a], ids[b]), jnp.where(keep, ids[b], ids[a])
    return vals, ids


def _merge_lists(vals, ids, k):
    neg = jnp.full_like(vals[0], -jnp.inf)
    vals = list(vals) + [neg]
    ids = list(ids) + [ids[0]]
    out_v, out_i = [], []
    tie = jnp.zeros_like(vals[0])
    for it in range(k + 1):
        m = _all_sublanes(jnp.maximum, vals[0])
        idw = _all_sublanes(jnp.minimum, jnp.where(vals[0] == m, ids[0], ID_SENTINEL))
        if out_v:
            tie = jnp.where(m == out_v[-1], 1.0, tie)
        out_v.append(m)
        out_i.append(idw)
        win = ids[0] == idw
        for r in range(min(k - it, len(vals) - 1)):
            vals[r] = jnp.where(win, vals[r + 1], vals[r])
            ids[r] = jnp.where(win, ids[r + 1], ids[r])
    return out_v, out_i, tie


def _rows_to_tile(rows):
    sub = lax.broadcasted_iota(jnp.int32, rows[0].shape, 0)
    out = rows[-1]
    for j in range(len(rows) - 2, -1, -1):
        out = jnp.where(sub == j, rows[j], out)
    return out


def _pair_merge(v1, i1, v2, i2, k, n_keys):
    g = SUBLANES
    n_row = 4
    assert k == 16 and g == 8 and k // (n_row + 1) < n_row <= k // n_row
    n_col = k // (n_row + 1)
    lens = [k // (r + 1) for r in range(n_row)] + [k // (c + 1) - n_row for c in range(n_col)]
    sub = lax.broadcasted_iota(jnp.int32, v1[0].shape, 0)
    is_row = sub < n_row
    length = jnp.zeros(v1[0].shape, F32)
    for l, ln in enumerate(lens):
        length = jnp.where(sub == l, float(ln), length)
    pick = lambda rows, lo: _rows_to_tile([rows[max(j - lo, 0)] if j - lo < len(rows) else rows[-1]
                                            for j in range(g)])
    v1_row, i1_row = pick(v1[:n_row], 0), pick(i1[:n_row], 0)
    v2_col, i2_col = pick(v2[:n_col], n_row), pick(i2[:n_col], n_row)
    vals, eids = [], []
    for q in range(k):
        rq = min(n_row + q, k - 1)
        val = jnp.where(is_row, v1_row, v1[rq]) + jnp.where(is_row, v2[q], v2_col)
        eid = jnp.where(is_row, i1_row, i1[rq]) * float(n_keys) + jnp.where(is_row, i2[q], i2_col)
        vals.append(jnp.where(length > float(q), val, -jnp.inf))
        eids.append(eid)
    return _merge_lists(vals, eids, k)


def _route_kernel(x_ref, g_ref, wq_ref, k1_ref, k2_ref, xn_ref, e_ref, gate_ref, xn_s, q_s):
    h = pl.program_id(1)
    tm = x_ref.shape[0]
    nsub = tm // LANES

    @pl.when(h == 0)
    def _():
        xn = _rms(x_ref[...], g_ref[...]).astype(BF16)
        xn_s[...] = xn
        xn_ref[...] = xn

    q = jnp.dot(xn_s[...], wq_ref[...], preferred_element_type=F32)
    q_s[...] = q.astype(BF16).reshape(nsub, LANES, 2 * PEER_D_HALF)
    k1 = k1_ref[0]
    k2 = k2_ref[0]

    kk, g = PEER_TOPK, SUBLANES

    def scores(c):
        qc = q_s[c]
        s1 = lax.dot_general(k1, qc[:, :PEER_D_HALF], _NT, preferred_element_type=F32)
        s2 = lax.dot_general(k2, qc[:, PEER_D_HALF:], _NT, preferred_element_type=F32)
        return s1, s2

    def fast(c):
        s1, s2 = scores(c)
        v1, i1, tie1 = _merge_lists(*_sort_classes(s1), kk)
        v2, i2, tie2 = _merge_lists(*_sort_classes(s2), kk)
        tops, eids, tie3 = _pair_merge(v1, i1, v2, i2, kk, PEER_N_KEYS)
        top = jnp.concatenate([_rows_to_tile(tops[r:r + g]) for r in range(0, kk, g)], axis=0)
        eid = jnp.concatenate([_rows_to_tile(eids[r:r + g]) for r in range(0, kk, g)], axis=0)
        ex = jnp.exp(top - tops[0][:1])
        gate_ref[0, c] = ex / jnp.sum(ex, axis=0, keepdims=True)
        e_ref[0, c] = eid.astype(jnp.int32)
        return jnp.max(jnp.maximum(jnp.maximum(tie1, tie2), tie3))

    def exact(c):
        s1, s2 = scores(c)
        v1, i1 = _topk_rows(s1, kk)
        v2, i2 = _topk_rows(s2, kk)
        tops, eids = _pair_topk(v1, i1, v2, i2, kk, PEER_N_KEYS)
        top = jnp.concatenate(tops, axis=0)
        ex = jnp.exp(top - tops[0])
        gate_ref[0, c] = ex / jnp.sum(ex, axis=0, keepdims=True)
        e_ref[0, c] = jnp.concatenate(eids, axis=0).astype(jnp.int32)

    def sub_blocks(cg, carry):
        cs = [cg * ROUTE_GROUP + u for u in range(ROUTE_GROUP)]
        ties = [fast(c) for c in cs]
        for c, tie in zip(cs, ties):
            pl.when(tie > 0.0)(functools.partial(exact, c))
        return carry

    lax.fori_loop(0, nsub // ROUTE_GROUP, sub_blocks, 0)


def _peer_route(x2d, g, w_q, k1, k2):
    t, d = x2d.shape
    tm = _tile(t, 512)
    nsub = tm // LANES
    hq = 2 * PEER_D_HALF
    xn, eid, gate = pl.pallas_call(
        _route_kernel,
        grid=(t // tm, PEER_HEADS),
        in_specs=[pl.BlockSpec((tm, d), lambda i, h: (i, 0)),
                  pl.BlockSpec((1, d), lambda i, h: (0, 0)),
                  pl.BlockSpec((d, hq), lambda i, h: (0, h)),
                  pl.BlockSpec((1, PEER_N_KEYS, PEER_D_HALF), lambda i, h: (h, 0, 0)),
                  pl.BlockSpec((1, PEER_N_KEYS, PEER_D_HALF), lambda i, h: (h, 0, 0))],
        out_specs=[pl.BlockSpec((tm, d), lambda i, h: (i, 0)),
                   pl.BlockSpec((1, nsub, PEER_TOPK, LANES), lambda i, h: (h, i, 0, 0)),
                   pl.BlockSpec((1, nsub, PEER_TOPK, LANES), lambda i, h: (h, i, 0, 0))],
        out_shape=[jax.ShapeDtypeStruct((t, d), BF16),
                   jax.ShapeDtypeStruct((PEER_HEADS, t // LANES, PEER_TOPK, LANES), jnp.int32),
                   jax.ShapeDtypeStruct((PEER_HEADS, t // LANES, PEER_TOPK, LANES), F32)],
        scratch_shapes=[pltpu.VMEM((tm, d), BF16), pltpu.VMEM((nsub, LANES, hq), BF16)],
        compiler_params=_params("parallel", "arbitrary"),
        name="peer_route",
    )(x2d, g.reshape(1, d), w_q.astype(BF16), k1.astype(BF16), k2.astype(BF16))
    return xn, eid, gate


def _pack_bf16_pair(a, b):
    abits = lax.bitcast_convert_type(a.astype(BF16).astype(F32), jnp.uint32)
    bbits = lax.bitcast_convert_type(b.astype(BF16).astype(F32), jnp.uint32)
    return (abits >> 16) | (bbits & jnp.uint32(0xFFFF0000))


def _unpack_bf16_pair(w):
    lo = lax.bitcast_convert_type(w << 16, F32)
    hi = lax.bitcast_convert_type(w & jnp.uint32(0xFFFF0000), F32)
    return lo, hi


def _gate_matrix_kernel(e_ref, g_ref, o_ref):
    pb = e_ref.shape[0]
    nk = PEER_N_KEYS
    lane = lax.broadcasted_iota(jnp.int32, (1, 2 * nk), 1)
    second = jnp.where(lane >= nk, nk, 0)
    sub1 = lax.broadcasted_iota(jnp.int32, (nk, 2 * nk), 0)
    sub2 = lax.broadcasted_iota(jnp.int32, (2 * nk, 2 * nk), 0)

    def pair(p, carry):
        e = e_ref[pl.ds(p, 1), :]
        g = g_ref[pl.ds(p, 1), :]
        i1 = e >> 7
        i2 = (e & (nk - 1)) + second
        lhs = jnp.where(sub1 == i1, g, 0.0).astype(BF16)
        rhs = jnp.where(sub2 == i2, 1.0, 0.0).astype(BF16)
        res = lax.dot_general(lhs, rhs, _NT, preferred_element_type=F32)
        packed = _pack_bf16_pair(res[:, :nk], res[:, nk:])
        for c in range(nk // SUBLANES):
            o_ref[c, p] = packed[c * SUBLANES:(c + 1) * SUBLANES, :]
        return carry

    lax.fori_loop(0, pb, pair, 0, unroll=GATE_UNROLL)


def _gate_matrix(eid, gate, tm):
    hh, nb, kk, ln = eid.shape
    t = nb * ln
    nsel = hh * kk
    half = tm // 2
    nchunk = PEER_N_KEYS // SUBLANES
    pb = _tile(half, 64)

    def pairs(a):
        a = a.transpose(1, 3, 0, 2).reshape(t // tm, 2, half, nsel)
        return a.transpose(0, 2, 1, 3).reshape(t // 2, 2 * nsel)

    spt = half // pb
    wp = pl.pallas_call(
        _gate_matrix_kernel,
        grid=(t // tm, spt),
        in_specs=[pl.BlockSpec((pb, 2 * nsel), lambda i, s: (i * spt + s, 0)),
                  pl.BlockSpec((pb, 2 * nsel), lambda i, s: (i * spt + s, 0))],
        out_specs=pl.BlockSpec((nchunk, pb, SUBLANES, PEER_N_KEYS), lambda i, s: (i, s, 0, 0)),
        out_shape=jax.ShapeDtypeStruct((t // tm * nchunk, half, SUBLANES, PEER_N_KEYS), jnp.uint32),
        compiler_params=_params("parallel", "parallel"),
        name="peer_gate_matrix",
    )(pairs(eid), pairs(gate))
    return wp.reshape(t // tm * nchunk * half * SUBLANES, PEER_N_KEYS)


def _experts_kernel(final, xn_ref, ut_ref, v_ref, w_ref, x_ref, fg_ref, o_ref, acc_ref):
    j = pl.program_id(1)
    half = x_ref.shape[0] // 2

    @pl.when(j == 0)
    def _():
        acc_ref[...] = x_ref[...]

    a = jnp.dot(xn_ref[...], ut_ref[...], preferred_element_type=F32)
    cols = []
    for chunk in range(EXPERT_CHUNKS):
        for c in range(SUBLANES):
            wc = w_ref[pl.ds(chunk * half * SUBLANES + c, half, stride=SUBLANES), :]
            cols.append(jnp.concatenate(_unpack_bf16_pair(wc), axis=0))
    w = jnp.concatenate(cols, axis=1)
    gelu = 0.5 * a * (1.0 + lax.erf(a * INV_SQRT2))
    hid = (gelu * w).astype(BF16)
    acc_ref[...] += jnp.dot(hid, v_ref[...], preferred_element_type=F32)

    @pl.when(j == pl.num_programs(1) - 1)
    def _():
        y = acc_ref[...]
        o_ref[...] = _rms(y, fg_ref[...]) if final else y


def _peer_experts(x2d, xn, w, u_t, v, final_gain, tm):
    t, d = x2d.shape
    ne = v.shape[0]
    te = EXPERT_CHUNKS * SUBLANES * PEER_N_KEYS
    nj = ne // te
    final = final_gain is not None
    fg = (final_gain if final else jnp.ones((d,), F32)).reshape(1, d)
    return pl.pallas_call(
        functools.partial(_experts_kernel, final),
        grid=(t // tm, nj),
        in_specs=[pl.BlockSpec((tm, d), lambda i, j: (i, 0)),
                  pl.BlockSpec((d, te), lambda i, j: (0, j)),
                  pl.BlockSpec((te, d), lambda i, j: (j, 0)),
                  pl.BlockSpec((EXPERT_CHUNKS * tm // 2 * SUBLANES, PEER_N_KEYS), lambda i, j: (i * nj + j, 0)),
                  pl.BlockSpec((tm, d), lambda i, j: (i, 0)),
                  pl.BlockSpec((1, d), lambda i, j: (0, 0))],
        out_specs=pl.BlockSpec((tm, d), lambda i, j: (i, 0)),
        out_shape=jax.ShapeDtypeStruct((t, d), F32),
        scratch_shapes=[pltpu.VMEM((tm, d), F32)],
        compiler_params=_params("parallel", "arbitrary"),
        name="peer_experts",
    )(xn, u_t, v, w, x2d, fg)


def _peer_ffn(x2d, g, w_q, k1, k2, u_tab, v_tab, final_gain=None):
    tm = _tile(x2d.shape[0], 512)
    xn, eid, gate = _peer_route(x2d, g, w_q, k1, k2)
    w = _gate_matrix(eid, gate, tm)
    return _peer_experts(x2d, xn, w, u_tab.astype(BF16).T, v_tab.astype(BF16), final_gain, tm)


def _mla_proj_kernel(scale, x_ref, g_ref, win_ref, qg_ref, kvg_ref, wqm_ref, wqr_ref, wk_ref, wv_ref,
                     cos_ref, sin_ref, q_ref, k_ref, v_ref):
    xn = _rms(x_ref[...], g_ref[...]).astype(BF16)
    lat = jnp.dot(xn, win_ref[...], preferred_element_type=F32)
    nq, nkv = MLA_Q_LORA, MLA_KV_LORA
    c_q = _rms(lat[:, :nq], qg_ref[...]).astype(BF16)
    c_kv = _rms(lat[:, nq:nq + nkv], kvg_ref[...]).astype(BF16)
    cos = cos_ref[...]
    sin = sin_ref[...]
    k_rope = lat[:, nq + nkv:nq + nkv + HEAD_SLAB] * cos + lat[:, nq + nkv + HEAD_SLAB:] * sin
    qm = jnp.dot(c_q, wqm_ref[...], preferred_element_type=F32)
    qr = jnp.dot(c_q, wqr_ref[...], preferred_element_type=F32)
    kn = jnp.dot(c_kv, wk_ref[...], preferred_element_type=F32)
    v_ref[...] = jnp.dot(c_kv, wv_ref[...], preferred_element_type=F32).astype(v_ref.dtype)
    for h in range(MLA_HEADS):
        sl = slice(h * HEAD_SLAB, (h + 1) * HEAD_SLAB)
        q_ref[:, sl] = ((qm[:, sl] * cos + qr[:, sl] * sin) * scale).astype(q_ref.dtype)
        k_ref[:, sl] = (kn[:, sl] + k_rope).astype(k_ref.dtype)


def _attn_kernel(q_ref, k_ref, v_ref, o_ref):
    v2 = v_ref[...]
    outs = []
    for hh in range(2):
        sl = slice(hh * HEAD_SLAB, (hh + 1) * HEAD_SLAB)
        s = lax.dot_general(q_ref[:, sl], k_ref[:, sl], _NT, preferred_element_type=F32)
        m = jnp.max(s, axis=-1, keepdims=True)
        p = jnp.exp(s - m)
        l = jnp.sum(p, axis=-1, keepdims=True)
        pv = jnp.dot(p.astype(BF16), v2, preferred_element_type=F32)
        outs.append(pv / l)
    lane = lax.broadcasted_iota(jnp.int32, outs[0].shape, 1)
    o_ref[...] = jnp.where(lane < MLA_V, outs[0], outs[1]).astype(o_ref.dtype)


def _out_proj_kernel(o_ref, w_ref, x_ref, y_ref):
    y_ref[...] = x_ref[...] + jnp.dot(o_ref[...], w_ref[...], preferred_element_type=F32)


def _rot_half(w):
    hr = MLA_ROPE // 2
    return jnp.concatenate([-w[..., hr:], w[..., :hr]], axis=-1)


def _mla_mixer(x2d, bsz, seq, g, w_in, q_norm, kv_norm, w_uq, w_ukv, w_o):
    t, d = x2d.shape
    nh, nq, nkv = MLA_HEADS, MLA_Q_LORA, MLA_KV_LORA
    pad = HEAD_SLAB - MLA_NOPE - MLA_ROPE

    w_kr = w_in[:, nq + nkv:]
    zl = jnp.zeros((d, MLA_NOPE), F32)
    zr = jnp.zeros((d, pad), F32)
    win_ext = jnp.concatenate([w_in[:, :nq + nkv], zl, w_kr, zr, zl, _rot_half(w_kr), zr], axis=1)
    wq = w_uq.reshape(nq, nh, MLA_NOPE + MLA_ROPE)
    zq = jnp.zeros((nq, nh, pad), F32)
    wq_main = jnp.concatenate([wq, zq], axis=-1).reshape(nq, nh * HEAD_SLAB)
    wq_rot = jnp.concatenate([jnp.zeros((nq, nh, MLA_NOPE), F32), _rot_half(wq[..., MLA_NOPE:]), zq],
                             axis=-1).reshape(nq, nh * HEAD_SLAB)
    wkv = w_ukv.reshape(nkv, nh, MLA_NOPE + MLA_V)
    wk = jnp.concatenate([wkv[..., :MLA_NOPE], jnp.zeros((nkv, nh, HEAD_SLAB - MLA_NOPE), F32)],
                         axis=-1).reshape(nkv, nh * HEAD_SLAB)
    wv = wkv[..., MLA_NOPE:].reshape(nkv, nh * MLA_V)

    pos = jnp.arange(seq, dtype=F32)
    inv_freq = ROPE_THETA ** (-jnp.arange(0, MLA_ROPE, 2, dtype=F32) / MLA_ROPE)
    ang = pos[:, None] * inv_freq[None, :]
    cos, sin = jnp.cos(ang), jnp.sin(ang)
    cos_t = jnp.concatenate([jnp.ones((seq, MLA_NOPE), F32), cos, cos, jnp.zeros((seq, pad), F32)], axis=1)
    sin_t = jnp.concatenate([jnp.zeros((seq, MLA_NOPE), F32), sin, sin, jnp.zeros((seq, pad), F32)], axis=1)

    tm = _tile(seq, 512)
    spt = seq // tm
    row = lambda i: (i, 0)
    fixed = lambda i: (0, 0)
    nlat = win_ext.shape[1]
    q, k, v = pl.pallas_call(
        functools.partial(_mla_proj_kernel, float((MLA_NOPE + MLA_ROPE) ** -0.5)),
        grid=(t // tm,),
        in_specs=[pl.BlockSpec((tm, d), row), pl.BlockSpec((1, d), fixed),
                  pl.BlockSpec((d, nlat), fixed),
                  pl.BlockSpec((1, nq), fixed), pl.BlockSpec((1, nkv), fixed),
                  pl.BlockSpec((nq, nh * HEAD_SLAB), fixed), pl.BlockSpec((nq, nh * HEAD_SLAB), fixed),
                  pl.BlockSpec((nkv, nh * HEAD_SLAB), fixed), pl.BlockSpec((nkv, nh * MLA_V), fixed),
                  pl.BlockSpec((tm, HEAD_SLAB), lambda i: (i % spt, 0)),
                  pl.BlockSpec((tm, HEAD_SLAB), lambda i: (i % spt, 0))],
        out_specs=[pl.BlockSpec((tm, nh * HEAD_SLAB), row), pl.BlockSpec((tm, nh * HEAD_SLAB), row),
                   pl.BlockSpec((tm, nh * MLA_V), row)],
        out_shape=[jax.ShapeDtypeStruct((t, nh * HEAD_SLAB), BF16),
                   jax.ShapeDtypeStruct((t, nh * HEAD_SLAB), BF16),
                   jax.ShapeDtypeStruct((t, nh * MLA_V), BF16)],
        compiler_params=_params("parallel"),
        name="mla_proj",
    )(x2d, g.reshape(1, d), win_ext.astype(BF16), q_norm.reshape(1, nq), kv_norm.reshape(1, nkv),
      wq_main.astype(BF16), wq_rot.astype(BF16), wk.astype(BF16), wv.astype(BF16), cos_t, sin_t)

    tq = _tile(seq, 1024)
    qpt = seq // tq
    o = pl.pallas_call(
        _attn_kernel,
        grid=(bsz, nh // 2, qpt),
        in_specs=[pl.BlockSpec((tq, 2 * HEAD_SLAB), lambda b, hp, qi: (b * qpt + qi, hp)),
                  pl.BlockSpec((seq, 2 * HEAD_SLAB), lambda b, hp, qi: (b, hp)),
                  pl.BlockSpec((seq, 2 * MLA_V), lambda b, hp, qi: (b, hp))],
        out_specs=pl.BlockSpec((tq, 2 * MLA_V), lambda b, hp, qi: (b * qpt + qi, hp)),
        out_shape=jax.ShapeDtypeStruct((t, nh * MLA_V), BF16),
        compiler_params=_params("parallel", "parallel", "arbitrary"),
        name="mla_attention",
    )(q, k, v)

    tmo = _tile(t, 1024)
    return pl.pallas_call(
        _out_proj_kernel,
        grid=(t // tmo,),
        in_specs=[pl.BlockSpec((tmo, nh * MLA_V), row), pl.BlockSpec((nh * MLA_V, d), fixed),
                  pl.BlockSpec((tmo, d), row)],
        out_specs=pl.BlockSpec((tmo, d), row),
        out_shape=jax.ShapeDtypeStruct((t, d), F32),
        compiler_params=_params("parallel"),
        name="mla_out_proj",
    )(o, w_o.astype(BF16), x2d)


def kernel(x, norm_mix, norm_ffn, conv_w_in, conv_w, conv_w_out, mla_w_in, mla_q_norm, mla_kv_norm,
           mla_w_uq, mla_w_ukv, mla_w_o, peer_w_q, peer_k1, peer_k2, peer_u, peer_v, final_norm):
    bsz, seq, d = x.shape
    depth = norm_mix.shape[0]
    h = x.reshape(bsz * seq, d)
    for i in range(depth):
        j = i // 2
        if i % 2 == 0:
            h = _conv_mixer(h, seq, norm_mix[i], conv_w_in[j], conv_w[j], conv_w_out[j])
        else:
            h = _mla_mixer(h, bsz, seq, norm_mix[i], mla_w_in[j], mla_q_norm[j], mla_kv_norm[j],
                           mla_w_uq[j], mla_w_ukv[j], mla_w_o[j])
        h = _peer_ffn(h, norm_ffn[i], peer_w_q[i], peer_k1[i], peer_k2[i], peer_u[i], peer_v[i],
                      final_norm if i == depth - 1 else None)
    return h.reshape(bsz, seq, d)
```

```python
import functools

import jax
import jax.numpy as jnp
from jax import lax
from jax.experimental import pallas as pl
from jax.experimental.pallas import tpu as pltpu

F32 = jnp.float32
BF16 = jnp.bfloat16

NORM_EPS = 1e-6
CONV_WIDTH = 3
MLA_HEADS = 16
MLA_Q_LORA = 384
MLA_KV_LORA = 256
MLA_NOPE = 64
MLA_ROPE = 32
MLA_V = 64
ROPE_THETA = 10000.0
PEER_HEADS = 8
PEER_N_KEYS = 128
PEER_TOPK = 16
PEER_D_HALF = 128

LANES = 128
SUBLANES = 8
HEAD_SLAB = 128
VMEM_LIMIT = 56 * 1024 * 1024
EXPERT_CHUNKS = 2
ROUTE_GROUP = 4
ATTN_ROWS = 256
GATE_UNROLL = 32

INV_SQRT2 = 0.7071067811865476
ID_SENTINEL = 1e9
_NT = (((1,), (1,)), ((), ()))


def _tile(n, pref):
    t = min(n, pref)
    while n % t:
        t -= 1
    return t


def _params(*sem):
    return pltpu.CompilerParams(dimension_semantics=sem, vmem_limit_bytes=VMEM_LIMIT)


def _rms(x, g):
    return x * lax.rsqrt(jnp.mean(x * x, axis=-1, keepdims=True) + NORM_EPS) * g


def _conv_in_kernel(x_ref, g_ref, w_ref, u_ref, b_ref):
    d = x_ref.shape[1]
    xn = _rms(x_ref[...], g_ref[...]).astype(BF16)
    b_ref[...] = jnp.dot(xn, w_ref[:, :d], preferred_element_type=F32)
    c = jnp.dot(xn, w_ref[:, d:2 * d], preferred_element_type=F32)
    h = jnp.dot(xn, w_ref[:, 2 * d:], preferred_element_type=F32)
    u_ref[...] = c * h


def _conv_out_kernel(seq_tiles, u_ref, up_ref, un_ref, b_ref, x_ref, cw_ref, w_ref, o_ref):
    i = pl.program_id(0)
    tm = u_ref.shape[0]
    u = u_ref[...]
    first = (i % seq_tiles) == 0
    last = (i % seq_tiles) == seq_tiles - 1
    prev_row = jnp.where(first, 0.0, up_ref[SUBLANES - 1:SUBLANES, :])
    next_row = jnp.where(last, 0.0, un_ref[0:1, :])
    row = lax.broadcasted_iota(jnp.int32, u.shape, 0)
    u_m = jnp.where(row == 0, prev_row, pltpu.roll(u, 1, 0))
    u_p = jnp.where(row == tm - 1, next_row, pltpu.roll(u, tm - 1, 0))
    y = cw_ref[0:1, :] * u_m + cw_ref[1:2, :] * u + cw_ref[2:3, :] * u_p
    z = (b_ref[...] * y).astype(BF16)
    o_ref[...] = x_ref[...] + jnp.dot(z, w_ref[...], preferred_element_type=F32)


def _conv_mixer(x2d, seq, g, w_in, conv_w, w_out):
    t, d = x2d.shape
    tm = _tile(seq, 512)
    nt = t // tm
    row = lambda i: (i, 0)
    fixed = lambda i: (0, 0)
    u, b = pl.pallas_call(
        _conv_in_kernel,
        grid=(nt,),
        in_specs=[pl.BlockSpec((tm, d), row), pl.BlockSpec((1, d), fixed),
                  pl.BlockSpec((d, 3 * d), fixed)],
        out_specs=[pl.BlockSpec((tm, d), row), pl.BlockSpec((tm, d), row)],
        out_shape=[jax.ShapeDtypeStruct((t, d), F32)] * 2,
        compiler_params=_params("parallel"),
        name="conv_in",
    )(x2d, g.reshape(1, d), w_in.astype(BF16))
    r8 = tm // SUBLANES
    n8 = t // SUBLANES
    return pl.pallas_call(
        functools.partial(_conv_out_kernel, seq // tm),
        grid=(nt,),
        in_specs=[pl.BlockSpec((tm, d), row),
                  pl.BlockSpec((SUBLANES, d), lambda i: (jnp.maximum(i * r8 - 1, 0), 0)),
                  pl.BlockSpec((SUBLANES, d), lambda i: (jnp.minimum((i + 1) * r8, n8 - 1), 0)),
                  pl.BlockSpec((tm, d), row), pl.BlockSpec((tm, d), row),
                  pl.BlockSpec((CONV_WIDTH, d), fixed), pl.BlockSpec((d, d), fixed)],
        out_specs=pl.BlockSpec((tm, d), row),
        out_shape=jax.ShapeDtypeStruct((t, d), F32),
        compiler_params=_params("parallel"),
        name="conv_out",
    )(u, u, u, b, x2d, conv_w, w_out.astype(BF16))


def _topk_rows(s, k):
    n = s.shape[0]
    iota = lax.broadcasted_iota(jnp.int32, s.shape, 0).astype(F32)
    vals, idxs = [], []
    for _ in range(k):
        m = jnp.max(s, axis=0, keepdims=True)
        idx = jnp.min(jnp.where(s == m, iota, float(n)), axis=0, keepdims=True)
        vals.append(m)
        idxs.append(idx)
        s = jnp.where(iota == idx, -jnp.inf, s)
    return vals, idxs


def _pair_topk(v1, i1, v2, i2, k, n_keys):
    g = SUBLANES
    w = lax.broadcasted_iota(jnp.int32, (g, v1[0].shape[1]), 0)
    wf = w.astype(F32)
    cat = lambda rows: jnp.concatenate(rows, axis=0)
    neg = -jnp.inf
    cands, poss, eids = [], [], []
    for r in range(k):
        cmax = k // (r + 1)
        if cmax < 4:
            break
        for c0 in range(0, cmax, g):
            val = v1[r] + cat(v2[c0:c0 + g])
            eid = i1[r] * float(n_keys) + cat(i2[c0:c0 + g])
            if c0 + g > cmax:
                val = jnp.where(w < cmax - c0, val, neg)
            cands.append(val)
            poss.append(wf + float(r * k + c0))
            eids.append(eid)
    r_done = r
    for c in range(k):
        rmax = k // (c + 1)
        if rmax <= r_done:
            break
        for r0 in range((r_done // g) * g, rmax, g):
            val = cat(v1[r0:r0 + g]) + v2[c]
            eid = cat(i1[r0:r0 + g]) * float(n_keys) + i2[c]
            lo, hi = max(r_done - r0, 0), min(rmax - r0, g)
            if lo > 0 or hi < g:
                val = jnp.where((w >= lo) & (w < hi), val, neg)
            cands.append(val)
            poss.append(wf * float(k) + float(r0 * k + c))
            eids.append(eid)
    cand, pos, eid = cat(cands), cat(poss), cat(eids)
    tops, sel_e = [], []
    for _ in range(k):
        m = jnp.max(cand, axis=0, keepdims=True)
        p = jnp.min(jnp.where(cand == m, pos, float(k * k)), axis=0, keepdims=True)
        hit = pos == p
        tops.append(m)
        sel_e.append(jnp.max(jnp.where(hit, eid, -1.0), axis=0, keepdims=True))
        cand = jnp.where(hit, neg, cand)
    return tops, sel_e


def _batcher_pairs(n):
    pairs = []
    p = 1
    while p < n:
        k = p
        while k >= 1:
            for j in range(k % p, n - k, 2 * k):
                for i in range(min(k, n - j - k)):
                    if (i + j) // (2 * p) == (i + j + k) // (2 * p):
                        pairs.append((i + j, i + j + k))
            k //= 2
        p *= 2
    return pairs


def _all_sublanes(op, x):
    for shift in (4, 2, 1):
        x = op(x, pltpu.roll(x, shift, 0))
    return x


def _sort_classes(s):
    g = SUBLANES
    n = s.shape[0] // g
    sub = lax.broadcasted_iota(jnp.int32, (g, s.shape[1]), 0).astype(F32)
    vals = [s[v * g:(v + 1) * g, :] for v in range(n)]
    ids = [sub + float(v * g) for v in range(n)]
    for a, b in _batcher_pairs(n):
        keep = vals[a] >= vals[b]
        vals[a], vals[b] = jnp.maximum(vals[a], vals[b]), jnp.minimum(vals[a], vals[b])
        ids[a], ids[b] = jnp.where(keep, ids[a], ids[b]), jnp.where(keep, ids[b], ids[a])
    return vals, ids


def _merge_lists(vals, ids, k):
    neg = jnp.full_like(vals[0], -jnp.inf)
    vals = list(vals) + [neg]
    ids = list(ids) + [ids[0]]
    out_v, out_i = [], []
    tie = jnp.zeros_like(vals[0])
    for it in range(k + 1):
        m = _all_sublanes(jnp.maximum, vals[0])
        idw = _all_sublanes(jnp.minimum, jnp.where(vals[0] == m, ids[0], ID_SENTINEL))
        if out_v:
            tie = jnp.where(m == out_v[-1], 1.0, tie)
        out_v.append(m)
        out_i.append(idw)
        win = ids[0] == idw
        for r in range(min(k - it, len(vals) - 1)):
            vals[r] = jnp.where(win, vals[r + 1], vals[r])
            ids[r] = jnp.where(win, ids[r + 1], ids[r])
    return out_v, out_i, tie


def _rows_to_tile(rows):
    sub = lax.broadcasted_iota(jnp.int32, rows[0].shape, 0)
    out = rows[-1]
    for j in range(len(rows) - 2, -1, -1):
        out = jnp.where(sub == j, rows[j], out)
    return out


def _pair_merge(v1, i1, v2, i2, k, n_keys):
    g = SUBLANES
    n_row = 4
    assert k == 16 and g == 8 and k // (n_row + 1) < n_row <= k // n_row
    n_col = k // (n_row + 1)
    lens = [k // (r + 1) for r in range(n_row)] + [k // (c + 1) - n_row for c in range(n_col)]
    sub = lax.broadcasted_iota(jnp.int32, v1[0].shape, 0)
    is_row = sub < n_row
    length = jnp.zeros(v1[0].shape, F32)
    for l, ln in enumerate(lens):
        length = jnp.where(sub == l, float(ln), length)
    pick = lambda rows, lo: _rows_to_tile([rows[max(j - lo, 0)] if j - lo < len(rows) else rows[-1]
                                            for j in range(g)])
    v1_row, i1_row = pick(v1[:n_row], 0), pick(i1[:n_row], 0)
    v2_col, i2_col = pick(v2[:n_col], n_row), pick(i2[:n_col], n_row)
    vals, eids = [], []
    for q in range(k):
        rq = min(n_row + q, k - 1)
        val = jnp.where(is_row, v1_row, v1[rq]) + jnp.where(is_row, v2[q], v2_col)
        eid = jnp.where(is_row, i1_row, i1[rq]) * float(n_keys) + jnp.where(is_row, i2[q], i2_col)
        vals.append(jnp.where(length > float(q), val, -jnp.inf))
        eids.append(eid)
    return _merge_lists(vals, eids, k)


def _route_kernel(x_ref, g_ref, wq_ref, k1_ref, k2_ref, xn_ref, e_ref, gate_ref, xn_s, q_s):
    h = pl.program_id(1)
    tm = x_ref.shape[0]
    nsub = tm // LANES

    @pl.when(h == 0)
    def _():
        xn = _rms(x_ref[...], g_ref[...]).astype(BF16)
        xn_s[...] = xn
        xn_ref[...] = xn

    q = jnp.dot(xn_s[...], wq_ref[...], preferred_element_type=F32)
    q_s[...] = q.astype(BF16).reshape(nsub, LANES, 2 * PEER_D_HALF)
    k1 = k1_ref[0]
    k2 = k2_ref[0]

    kk, g = PEER_TOPK, SUBLANES

    def scores(c):
        qc = q_s[c]
        s1 = lax.dot_general(k1, qc[:, :PEER_D_HALF], _NT, preferred_element_type=F32)
        s2 = lax.dot_general(k2, qc[:, PEER_D_HALF:], _NT, preferred_element_type=F32)
        return s1, s2

    def fast(c):
        s1, s2 = scores(c)
        v1, i1, tie1 = _merge_lists(*_sort_classes(s1), kk)
        v2, i2, tie2 = _merge_lists(*_sort_classes(s2), kk)
        tops, eids, tie3 = _pair_merge(v1, i1, v2, i2, kk, PEER_N_KEYS)
        top = jnp.concatenate([_rows_to_tile(tops[r:r + g]) for r in range(0, kk, g)], axis=0)
        eid = jnp.concatenate([_rows_to_tile(eids[r:r + g]) for r in range(0, kk, g)], axis=0)
        ex = jnp.exp(top - tops[0][:1])
        gate_ref[0, c] = ex / jnp.sum(ex, axis=0, keepdims=True)
        e_ref[0, c] = eid.astype(jnp.int32)
        return jnp.max(jnp.maximum(jnp.maximum(tie1, tie2), tie3))

    def exact(c):
        s1, s2 = scores(c)
        v1, i1 = _topk_rows(s1, kk)
        v2, i2 = _topk_rows(s2, kk)
        tops, eids = _pair_topk(v1, i1, v2, i2, kk, PEER_N_KEYS)
        top = jnp.concatenate(tops, axis=0)
        ex = jnp.exp(top - tops[0])
        gate_ref[0, c] = ex / jnp.sum(ex, axis=0, keepdims=True)
        e_ref[0, c] = jnp.concatenate(eids, axis=0).astype(jnp.int32)

    def sub_blocks(cg, carry):
        cs = [cg * ROUTE_GROUP + u for u in range(ROUTE_GROUP)]
        ties = [fast(c) for c in cs]
        for c, tie in zip(cs, ties):
            pl.when(tie > 0.0)(functools.partial(exact, c))
        return carry

    lax.fori_loop(0, nsub // ROUTE_GROUP, sub_blocks, 0)


def _peer_route(x2d, g, w_q, k1, k2):
    t, d = x2d.shape
    tm = _tile(t, 512)
    nsub = tm // LANES
    hq = 2 * PEER_D_HALF
    xn, eid, gate = pl.pallas_call(
        _route_kernel,
        grid=(t // tm, PEER_HEADS),
        in_specs=[pl.BlockSpec((tm, d), lambda i, h: (i, 0)),
                  pl.BlockSpec((1, d), lambda i, h: (0, 0)),
                  pl.BlockSpec((d, hq), lambda i, h: (0, h)),
                  pl.BlockSpec((1, PEER_N_KEYS, PEER_D_HALF), lambda i, h: (h, 0, 0)),
                  pl.BlockSpec((1, PEER_N_KEYS, PEER_D_HALF), lambda i, h: (h, 0, 0))],
        out_specs=[pl.BlockSpec((tm, d), lambda i, h: (i, 0)),
                   pl.BlockSpec((1, nsub, PEER_TOPK, LANES), lambda i, h: (h, i, 0, 0)),
                   pl.BlockSpec((1, nsub, PEER_TOPK, LANES), lambda i, h: (h, i, 0, 0))],
        out_shape=[jax.ShapeDtypeStruct((t, d), BF16),
                   jax.ShapeDtypeStruct((PEER_HEADS, t // LANES, PEER_TOPK, LANES), jnp.int32),
                   jax.ShapeDtypeStruct((PEER_HEADS, t // LANES, PEER_TOPK, LANES), F32)],
        scratch_shapes=[pltpu.VMEM((tm, d), BF16), pltpu.VMEM((nsub, LANES, hq), BF16)],
        compiler_params=_params("parallel", "arbitrary"),
        name="peer_route",
    )(x2d, g.reshape(1, d), w_q.astype(BF16), k1.astype(BF16), k2.astype(BF16))
    return xn, eid, gate


def _pack_bf16_pair(a, b):
    abits = lax.bitcast_convert_type(a.astype(BF16).astype(F32), jnp.uint32)
    bbits = lax.bitcast_convert_type(b.astype(BF16).astype(F32), jnp.uint32)
    return (abits >> 16) | (bbits & jnp.uint32(0xFFFF0000))


def _unpack_bf16_pair(w):
    lo = lax.bitcast_convert_type(w << 16, F32)
    hi = lax.bitcast_convert_type(w & jnp.uint32(0xFFFF0000), F32)
    return lo, hi


def _gate_matrix_kernel(e_ref, g_ref, o_ref):
    pb = e_ref.shape[0]
    nk = PEER_N_KEYS
    lane = lax.broadcasted_iota(jnp.int32, (1, 2 * nk), 1)
    second = jnp.where(lane >= nk, nk, 0)
    sub1 = lax.broadcasted_iota(jnp.int32, (nk, 2 * nk), 0)
    sub2 = lax.broadcasted_iota(jnp.int32, (2 * nk, 2 * nk), 0)

    def pair(p, carry):
        e = e_ref[pl.ds(p, 1), :]
        g = g_ref[pl.ds(p, 1), :]
        i1 = e >> 7
        i2 = (e & (nk - 1)) + second
        lhs = jnp.where(sub1 == i1, g, 0.0).astype(BF16)
        rhs = jnp.where(sub2 == i2, 1.0, 0.0).astype(BF16)
        res = lax.dot_general(lhs, rhs, _NT, preferred_element_type=F32)
        packed = _pack_bf16_pair(res[:, :nk], res[:, nk:])
        for c in range(nk // SUBLANES):
            o_ref[c, p] = packed[c * SUBLANES:(c + 1) * SUBLANES, :]
        return carry

    lax.fori_loop(0, pb, pair, 0, unroll=GATE_UNROLL)


def _gate_matrix(eid, gate, tm):
    hh, nb, kk, ln = eid.shape
    t = nb * ln
    nsel = hh * kk
    half = tm // 2
    nchunk = PEER_N_KEYS // SUBLANES
    pb = _tile(half, 64)

    def pairs(a):
        a = a.transpose(1, 3, 0, 2).reshape(t // tm, 2, half, nsel)
        return a.transpose(0, 2, 1, 3).reshape(t // 2, 2 * nsel)

    spt = half // pb
    wp = pl.pallas_call(
        _gate_matrix_kernel,
        grid=(t // tm, spt),
        in_specs=[pl.BlockSpec((pb, 2 * nsel), lambda i, s: (i * spt + s, 0)),
                  pl.BlockSpec((pb, 2 * nsel), lambda i, s: (i * spt + s, 0))],
        out_specs=pl.BlockSpec((nchunk, pb, SUBLANES, PEER_N_KEYS), lambda i, s: (i, s, 0, 0)),
        out_shape=jax.ShapeDtypeStruct((t // tm * nchunk, half, SUBLANES, PEER_N_KEYS), jnp.uint32),
        compiler_params=_params("parallel", "parallel"),
        name="peer_gate_matrix",
    )(pairs(eid), pairs(gate))
    return wp.reshape(t // tm * nchunk * half * SUBLANES, PEER_N_KEYS)


def _experts_kernel(final, xn_ref, ut_ref, v_ref, w_ref, x_ref, fg_ref, o_ref, acc_ref):
    j = pl.program_id(1)
    half = x_ref.shape[0] // 2

    @pl.when(j == 0)
    def _():
        acc_ref[...] = x_ref[...]

    a = jnp.dot(xn_ref[...], ut_ref[...], preferred_element_type=F32)
    cols = []
    for chunk in range(EXPERT_CHUNKS):
        for c in range(SUBLANES):
            wc = w_ref[pl.ds(chunk * half * SUBLANES + c, half, stride=SUBLANES), :]
            cols.append(jnp.concatenate(_unpack_bf16_pair(wc), axis=0))
    w = jnp.concatenate(cols, axis=1)
    gelu = 0.5 * a * (1.0 + lax.erf(a * INV_SQRT2))
    hid = (gelu * w).astype(BF16)
    acc_ref[...] += jnp.dot(hid, v_ref[...], preferred_element_type=F32)

    @pl.when(j == pl.num_programs(1) - 1)
    def _():
        y = acc_ref[...]
        o_ref[...] = _rms(y, fg_ref[...]) if final else y


def _peer_experts(x2d, xn, w, u_t, v, final_gain, tm):
    t, d = x2d.shape
    ne = v.shape[0]
    te = EXPERT_CHUNKS * SUBLANES * PEER_N_KEYS
    nj = ne // te
    final = final_gain is not None
    fg = (final_gain if final else jnp.ones((d,), F32)).reshape(1, d)
    return pl.pallas_call(
        functools.partial(_experts_kernel, final),
        grid=(t // tm, nj),
        in_specs=[pl.BlockSpec((tm, d), lambda i, j: (i, 0)),
                  pl.BlockSpec((d, te), lambda i, j: (0, j)),
                  pl.BlockSpec((te, d), lambda i, j: (j, 0)),
                  pl.BlockSpec((EXPERT_CHUNKS * tm // 2 * SUBLANES, PEER_N_KEYS), lambda i, j: (i * nj + j, 0)),
                  pl.BlockSpec((tm, d), lambda i, j: (i, 0)),
                  pl.BlockSpec((1, d), lambda i, j: (0, 0))],
        out_specs=pl.BlockSpec((tm, d), lambda i, j: (i, 0)),
        out_shape=jax.ShapeDtypeStruct((t, d), F32),
        scratch_shapes=[pltpu.VMEM((tm, d), F32)],
        compiler_params=_params("parallel", "arbitrary"),
        name="peer_experts",
    )(xn, u_t, v, w, x2d, fg)


def _peer_ffn(x2d, g, w_q, k1, k2, u_tab, v_tab, final_gain=None):
    tm = _tile(x2d.shape[0], 512)
    xn, eid, gate = _peer_route(x2d, g, w_q, k1, k2)
    w = _gate_matrix(eid, gate, tm)
    return _peer_experts(x2d, xn, w, u_tab.astype(BF16).T, v_tab.astype(BF16), final_gain, tm)


def _mla_proj_kernel(scale, x_ref, g_ref, win_ref, qg_ref, kvg_ref, wqm_ref, wqr_ref, wk_ref, wv_ref,
                     cos_ref, sin_ref, q_ref, k_ref, v_ref):
    xn = _rms(x_ref[...], g_ref[...]).astype(BF16)
    lat = jnp.dot(xn, win_ref[...], preferred_element_type=F32)
    nq, nkv = MLA_Q_LORA, MLA_KV_LORA
    c_q = _rms(lat[:, :nq], qg_ref[...]).astype(BF16)
    c_kv = _rms(lat[:, nq:nq + nkv], kvg_ref[...]).astype(BF16)
    cos = cos_ref[...]
    sin = sin_ref[...]
    k_rope = lat[:, nq + nkv:nq + nkv + HEAD_SLAB] * cos + lat[:, nq + nkv + HEAD_SLAB:] * sin
    qm = jnp.dot(c_q, wqm_ref[...], preferred_element_type=F32)
    qr = jnp.dot(c_q, wqr_ref[...], preferred_element_type=F32)
    kn = jnp.dot(c_kv, wk_ref[...], preferred_element_type=F32)
    v_ref[...] = jnp.dot(c_kv, wv_ref[...], preferred_element_type=F32).astype(v_ref.dtype)
    for h in range(MLA_HEADS):
        sl = slice(h * HEAD_SLAB, (h + 1) * HEAD_SLAB)
        q_ref[:, sl] = ((qm[:, sl] * cos + qr[:, sl] * sin) * scale).astype(q_ref.dtype)
        k_ref[:, sl] = (kn[:, sl] + k_rope).astype(k_ref.dtype)


def _attn_kernel(q_ref, k_ref, v_ref, o_ref):
    def row_chunk(r, carry):
        rows = pl.ds(pl.multiple_of(r * ATTN_ROWS, ATTN_ROWS), ATTN_ROWS)
        outs = []
        for hh in range(2):
            sl = slice(hh * HEAD_SLAB, (hh + 1) * HEAD_SLAB)
            s = lax.dot_general(q_ref[rows, sl], k_ref[:, sl], _NT, preferred_element_type=F32)
            m = jnp.max(s, axis=-1, keepdims=True)
            p = jnp.exp(s - m)
            l = jnp.sum(p, axis=-1, keepdims=True)
            pv = jnp.dot(p.astype(BF16), v_ref[...], preferred_element_type=F32)
            outs.append(pv / l)
        lane = lax.broadcasted_iota(jnp.int32, outs[0].shape, 1)
        o_ref[rows, :] = jnp.where(lane < MLA_V, outs[0], outs[1]).astype(o_ref.dtype)
        return carry

    lax.fori_loop(0, q_ref.shape[0] // ATTN_ROWS, row_chunk, 0)


def _out_proj_kernel(o_ref, w_ref, x_ref, y_ref):
    y_ref[...] = x_ref[...] + jnp.dot(o_ref[...], w_ref[...], preferred_element_type=F32)


def _rot_half(w):
    hr = MLA_ROPE // 2
    return jnp.concatenate([-w[..., hr:], w[..., :hr]], axis=-1)


def _mla_mixer(x2d, bsz, seq, g, w_in, q_norm, kv_norm, w_uq, w_ukv, w_o):
    t, d = x2d.shape
    nh, nq, nkv = MLA_HEADS, MLA_Q_LORA, MLA_KV_LORA
    pad = HEAD_SLAB - MLA_NOPE - MLA_ROPE

    w_kr = w_in[:, nq + nkv:]
    zl = jnp.zeros((d, MLA_NOPE), F32)
    zr = jnp.zeros((d, pad), F32)
    win_ext = jnp.concatenate([w_in[:, :nq + nkv], zl, w_kr, zr, zl, _rot_half(w_kr), zr], axis=1)
    wq = w_uq.reshape(nq, nh, MLA_NOPE + MLA_ROPE)
    zq = jnp.zeros((nq, nh, pad), F32)
    wq_main = jnp.concatenate([wq, zq], axis=-1).reshape(nq, nh * HEAD_SLAB)
    wq_rot = jnp.concatenate([jnp.zeros((nq, nh, MLA_NOPE), F32), _rot_half(wq[..., MLA_NOPE:]), zq],
                             axis=-1).reshape(nq, nh * HEAD_SLAB)
    wkv = w_ukv.reshape(nkv, nh, MLA_NOPE + MLA_V)
    wk = jnp.concatenate([wkv[..., :MLA_NOPE], jnp.zeros((nkv, nh, HEAD_SLAB - MLA_NOPE), F32)],
                         axis=-1).reshape(nkv, nh * HEAD_SLAB)
    wv = wkv[..., MLA_NOPE:].reshape(nkv, nh * MLA_V)

    pos = jnp.arange(seq, dtype=F32)
    inv_freq = ROPE_THETA ** (-jnp.arange(0, MLA_ROPE, 2, dtype=F32) / MLA_ROPE)
    ang = pos[:, None] * inv_freq[None, :]
    cos, sin = jnp.cos(ang), jnp.sin(ang)
    cos_t = jnp.concatenate([jnp.ones((seq, MLA_NOPE), F32), cos, cos, jnp.zeros((seq, pad), F32)], axis=1)
    sin_t = jnp.concatenate([jnp.zeros((seq, MLA_NOPE), F32), sin, sin, jnp.zeros((seq, pad), F32)], axis=1)

    tm = _tile(seq, 512)
    spt = seq // tm
    row = lambda i: (i, 0)
    fixed = lambda i: (0, 0)
    nlat = win_ext.shape[1]
    q, k, v = pl.pallas_call(
        functools.partial(_mla_proj_kernel, float((MLA_NOPE + MLA_ROPE) ** -0.5)),
        grid=(t // tm,),
        in_specs=[pl.BlockSpec((tm, d), row), pl.BlockSpec((1, d), fixed),
                  pl.BlockSpec((d, nlat), fixed),
                  pl.BlockSpec((1, nq), fixed), pl.BlockSpec((1, nkv), fixed),
                  pl.BlockSpec((nq, nh * HEAD_SLAB), fixed), pl.BlockSpec((nq, nh * HEAD_SLAB), fixed),
                  pl.BlockSpec((nkv, nh * HEAD_SLAB), fixed), pl.BlockSpec((nkv, nh * MLA_V), fixed),
                  pl.BlockSpec((tm, HEAD_SLAB), lambda i: (i % spt, 0)),
                  pl.BlockSpec((tm, HEAD_SLAB), lambda i: (i % spt, 0))],
        out_specs=[pl.BlockSpec((tm, nh * HEAD_SLAB), row), pl.BlockSpec((tm, nh * HEAD_SLAB), row),
                   pl.BlockSpec((tm, nh * MLA_V), row)],
        out_shape=[jax.ShapeDtypeStruct((t, nh * HEAD_SLAB), BF16),
                   jax.ShapeDtypeStruct((t, nh * HEAD_SLAB), BF16),
                   jax.ShapeDtypeStruct((t, nh * MLA_V), BF16)],
        compiler_params=_params("parallel"),
        name="mla_proj",
    )(x2d, g.reshape(1, d), win_ext.astype(BF16), q_norm.reshape(1, nq), kv_norm.reshape(1, nkv),
      wq_main.astype(BF16), wq_rot.astype(BF16), wk.astype(BF16), wv.astype(BF16), cos_t, sin_t)

    tq = _tile(seq, 1024)
    qpt = seq // tq
    o = pl.pallas_call(
        _attn_kernel,
        grid=(bsz, nh // 2, qpt),
        in_specs=[pl.BlockSpec((tq, 2 * HEAD_SLAB), lambda b, hp, qi: (b * qpt + qi, hp)),
                  pl.BlockSpec((seq, 2 * HEAD_SLAB), lambda b, hp, qi: (b, hp)),
                  pl.BlockSpec((seq, 2 * MLA_V), lambda b, hp, qi: (b, hp))],
        out_specs=pl.BlockSpec((tq, 2 * MLA_V), lambda b, hp, qi: (b * qpt + qi, hp)),
        out_shape=jax.ShapeDtypeStruct((t, nh * MLA_V), BF16),
        compiler_params=_params("parallel", "parallel", "arbitrary"),
        name="mla_attention",
    )(q, k, v)

    tmo = _tile(t, 1024)
    return pl.pallas_call(
        _out_proj_kernel,
        grid=(t // tmo,),
        in_specs=[pl.BlockSpec((tmo, nh * MLA_V), row), pl.BlockSpec((nh * MLA_V, d), fixed),
                  pl.BlockSpec((tmo, d), row)],
        out_specs=pl.BlockSpec((tmo, d), row),
        out_shape=jax.ShapeDtypeStruct((t, d), F32),
        compiler_params=_params("parallel"),
        name="mla_out_proj",
    )(o, w_o.astype(BF16), x2d)


def kernel(x, norm_mix, norm_ffn, conv_w_in, conv_w, conv_w_out, mla_w_in, mla_q_norm, mla_kv_norm,
           mla_w_uq, mla_w_ukv, mla_w_o, peer_w_q, peer_k1, peer_k2, peer_u, peer_v, final_norm):
    bsz, seq, d = x.shape
    depth = norm_mix.shape[0]
    h = x.reshape(bsz * seq, d)
    for i in range(depth):
        j = i // 2
        if i % 2 == 0:
            h = _conv_mixer(h, seq, norm_mix[i], conv_w_in[j], conv_w[j], conv_w_out[j])
        else:
            h = _mla_mixer(h, bsz, seq, norm_mix[i], mla_w_in[j], mla_q_norm[j], mla_kv_norm[j],
                           mla_w_uq[j], mla_w_ukv[j], mla_w_o[j])
        h = _peer_ffn(h, norm_ffn[i], peer_w_q[i], peer_k1[i], peer_k2[i], peer_u[i], peer_v[i],
                      final_norm if i == depth - 1 else None)
    return h.reshape(bsz, seq, d)
```
